```python
import math
import jax, jax.numpy as jnp
from jax import lax
import numpy as np

D_MODEL = 1024
BATCH = 8
SEQ = 4096
DEPTH = 4
DEC_BATCH = 4
DEC_SEQ = 4096
PAST_LEN = 128

F32 = jnp.float32
EPS = 1e-6
F_MIN_GAP = 1e-6
H_A = 4
DK_A = 128
DV_A = 128
W_A = H_A * DV_A
QKV_A = 2 * H_A * DK_A + W_A
CONV_K = 5
CHUNK_A = 64
H_B = 4
DK_B = 128
DV_B = 128
W_B = H_B * DV_B
CHUNK_B = 16
H_C = 4
D_C = 64
DV_C = 2 * D_C
W_C = H_C * DV_C
QK_C = H_C * 2 * D_C
ROPE_THETA = 500000.0
ROPE_DIM = D_C // 4
Q_BLOCK = 128
N_BRANCH = 3
SPLIT_SIZES = (QKV_A, W_A, 2 * H_A, 2 * H_A,
               H_B * DK_B, W_B, 2 * H_B * DK_B, W_B,
               QK_C, QK_C, W_C, W_C,
               N_BRANCH * D_MODEL)
N_IN = sum(SPLIT_SIZES)

kernel_name = 'hybrid_gdn_hgrn2_diffattn_encoder'


def split_cols(t, sizes):
    out, start = [], 0
    for s in sizes:
        out.append(t[..., start:start + s])
        start += s
    return out


def rms_norm(x, g):
    xf = x.astype(F32)
    y = xf * lax.rsqrt(jnp.mean(xf * xf, axis=-1, keepdims=True) + EPS)
    return (y * g.astype(F32)).astype(x.dtype)


def l2_norm(x):
    xf = x.astype(F32)
    return xf * lax.rsqrt(jnp.sum(xf * xf, axis=-1, keepdims=True) + EPS)


def flip(t):
    return jnp.flip(t, axis=1)


def to_chunks(t, c):
    b, s, h, d = t.shape
    return t.reshape(b, s // c, c, h, d).transpose(1, 0, 3, 2, 4)


def from_chunks(t):
    n, b, h, c, d = t.shape
    return t.transpose(1, 0, 3, 2, 4).reshape(b, n * c, h, d)


def masked_exp(diff, mask):
    return jnp.where(mask, jnp.exp(jnp.where(mask, diff, 0.0)), 0.0)


def short_conv_silu(x, w):
    y = lax.conv_general_dilated(x, w[:, None, :].astype(x.dtype), window_strides=(1,),
                                 padding=[((CONV_K - 1) // 2, CONV_K // 2)],
                                 dimension_numbers=('NWC', 'WIO', 'NWC'),
                                 feature_group_count=x.shape[-1])
    return jax.nn.silu(y)


def gated_delta_chunked(q, k, v, beta, g):
    bsz, _, nh, dk = k.shape
    dv = v.shape[-1]
    q, k, v = (to_chunks(t.astype(F32), CHUNK_A) for t in (q, k, v))
    beta = to_chunks(beta.astype(F32)[..., None], CHUNK_A)[..., 0]
    gc = jnp.cumsum(to_chunks(g.astype(F32)[..., None], CHUNK_A)[..., 0], axis=-1)
    incl = jnp.tril(jnp.ones((CHUNK_A, CHUNK_A), bool))
    strict = jnp.tril(jnp.ones((CHUNK_A, CHUNK_A), bool), -1)
    decay = masked_exp(gc[..., :, None] - gc[..., None, :], incl)
    kb = k * beta[..., None]
    m = jnp.where(strict, jnp.einsum('nbhid,nbhjd->nbhij', kb, k) * decay, 0.0)
    lhs = m + jnp.eye(CHUNK_A, dtype=F32)
    rhs = jnp.concatenate([v * beta[..., None], kb * jnp.exp(gc)[..., None]], axis=-1)
    sol = lax.linalg.triangular_solve(lhs, rhs, left_side=True, lower=True)
    u, w = sol[..., :dv], sol[..., dv:]
    qs = q * (dk ** -0.5)
    a_qk = jnp.einsum('nbhid,nbhjd->nbhij', qs, k) * decay
    q_dec = qs * jnp.exp(gc)[..., None]
    g_last = gc[..., -1]
    k_dec = k * jnp.exp(g_last[..., None] - gc)[..., None]

    def step(s, inp):
        u_c, w_c, qd_c, a_c, kd_c, gl_c = inp
        v_new = u_c - jnp.einsum('bhck,bhkv->bhcv', w_c, s)
        o = jnp.einsum('bhck,bhkv->bhcv', qd_c, s) + jnp.einsum('bhij,bhjv->bhiv', a_c, v_new)
        s = s * jnp.exp(gl_c)[..., None, None] + jnp.einsum('bhck,bhcv->bhkv', kd_c, v_new)
        return s, o

    s0 = jnp.zeros((bsz, nh, dk, dv), F32)
    _, o = lax.scan(step, s0, (u, w, q_dec, a_qk, k_dec, g_last))
    return from_chunks(o)


def hgrn2_chunked(q, k, v, log_f):
    bsz, _, nh, dk = q.shape
    dv = v.shape[-1]
    q, k, v, log_f = (to_chunks(t.astype(F32), CHUNK_B) for t in (q, k, v, log_f))
    gc = jnp.cumsum(log_f, axis=-2)
    qs = q * (dk ** -0.5)
    q_dec = qs * jnp.exp(gc)
    g_last = gc[..., -1, :]
    k_dec = k * jnp.exp(g_last[..., None, :] - gc)
    incl = jnp.tril(jnp.ones((CHUNK_B, CHUNK_B), bool))[:, :, None]

    def step(s, inp):
        qs_c, qd_c, k_c, kd_c, v_c, gc_c, gl_c = inp
        pair = masked_exp(gc_c[:, :, :, None, :] - gc_c[:, :, None, :, :], incl)
        a = jnp.einsum('bhik,bhjk,bhijk->bhij', qs_c, k_c, pair)
        o = jnp.einsum('bhck,bhkv->bhcv', qd_c, s) + jnp.einsum('bhij,bhjv->bhiv', a, v_c)
        s = s * jnp.exp(gl_c)[..., None] + jnp.einsum('bhck,bhcv->bhkv', kd_c, v_c)
        return s, o

    s0 = jnp.zeros((bsz, nh, dk, dv), F32)
    _, o = lax.scan(step, s0, (qs, q_dec, k, k_dec, v, gc, g_last))
    return from_chunks(o)


def rope_tables(seq):
    inv = 1.0 / (ROPE_THETA ** (jnp.arange(0, ROPE_DIM, 2, dtype=F32) / ROPE_DIM))
    ang = jnp.arange(seq, dtype=F32)[:, None] * inv[None, :]
    return jnp.cos(ang), jnp.sin(ang)


def partial_rope(x, cos, sin):
    half = ROPE_DIM // 2
    c = cos[None, :, None, None, :].astype(x.dtype)
    s = sin[None, :, None, None, :].astype(x.dtype)
    x1, x2 = x[..., :half], x[..., half:ROPE_DIM]
    return jnp.concatenate([x1 * c - x2 * s, x2 * c + x1 * s, x[..., ROPE_DIM:]], axis=-1)


def diff_attention(q, k, v, lam):
    bsz, seq, nh = q.shape[:3]
    nq = seq // Q_BLOCK
    qb = q.reshape(bsz, nq, Q_BLOCK, nh, 2, D_C).transpose(1, 0, 2, 3, 4, 5)

    def block(q_blk):
        s = jnp.einsum('bqhmd,bkhmd->bhmqk', q_blk, k, preferred_element_type=F32)
        p = jax.nn.softmax(s, axis=-1)
        wgt = p[:, :, 0] - lam * p[:, :, 1]
        return jnp.einsum('bhqk,bkhe->bqhe', wgt.astype(v.dtype), v)

    o = lax.map(block, qb)
    return o.transpose(1, 0, 2, 3, 4).reshape(bsz, seq, nh, DV_C)


def hgrn_lower_bounds(lb_logits):
    p = jax.nn.softmax(lb_logits.astype(F32), axis=1)
    return jnp.cumsum(p, axis=1) - p[:, :1]


def layer(x, l, lbs, cos, sin, norm_g, w_in, conv_w, a_log, dt_bias, gdn_norm_g, hgrn_norm_g,
          q_norm_g, k_norm_g, lam_p, subln_g, w_br_a, w_br_b, w_br_c, w_out):
    dt = x.dtype
    bsz, seq, _ = x.shape
    h = rms_norm(x, norm_g)
    proj = h @ w_in
    (a_qkv, a_z, a_b, a_a, b_q, b_i, b_f, b_z, c_q, c_k, c_v, c_z, gate_logits) = split_cols(proj, SPLIT_SIZES)

    a_qkv = short_conv_silu(a_qkv, conv_w)
    aq, ak, av = split_cols(a_qkv, (H_A * DK_A, H_A * DK_A, W_A))
    aq = l2_norm(aq.reshape(bsz, seq, H_A, DK_A))
    ak = l2_norm(ak.reshape(bsz, seq, H_A, DK_A))
    av = av.reshape(bsz, seq, H_A, DV_A)
    beta = jax.nn.sigmoid(a_b.astype(F32)).reshape(bsz, seq, 2, H_A)
    ga = -jnp.exp(a_log.astype(F32)) * jax.nn.softplus(a_a.astype(F32).reshape(bsz, seq, 2, H_A) + dt_bias.astype(F32))
    o_a = (gated_delta_chunked(aq, ak, av, beta[:, :, 0], ga[:, :, 0])
           + flip(gated_delta_chunked(flip(aq), flip(ak), flip(av), flip(beta[:, :, 1]), flip(ga[:, :, 1]))))
    y_a = (rms_norm(o_a, gdn_norm_g).reshape(bsz, seq, W_A) * jax.nn.silu(a_z.astype(F32))).astype(dt)

    bq = jax.nn.silu(b_q).reshape(bsz, seq, H_B, DK_B)
    bi = b_i.reshape(bsz, seq, H_B, DV_B)
    bf = b_f.astype(F32).reshape(bsz, seq, 2, H_B, DK_B)
    lb = lbs[:, l].reshape(2, H_B, DK_B)
    kk = (1.0 - lb) * jax.nn.sigmoid(-bf)
    log_f = jnp.log1p(-jnp.minimum(kk, 1.0 - F_MIN_GAP))
    o_b = (hgrn2_chunked(bq, kk[:, :, 0], bi, log_f[:, :, 0])
           + flip(hgrn2_chunked(flip(bq), flip(kk[:, :, 1]), flip(bi), flip(log_f[:, :, 1]))))
    y_b = (rms_norm(o_b, hgrn_norm_g).reshape(bsz, seq, W_B) * jax.nn.silu(b_z.astype(F32))).astype(dt)

    lambda_init = 0.8 - 0.6 * math.exp(-0.3 * l)
    lp = lam_p.astype(F32)
    lam = jnp.exp(jnp.sum(lp[0] * lp[1])) - jnp.exp(jnp.sum(lp[2] * lp[3])) + lambda_init
    cq = partial_rope(rms_norm(c_q.reshape(bsz, seq, H_C, 2, D_C), q_norm_g), cos, sin) * (D_C ** -0.5)
    ck = partial_rope(rms_norm(c_k.reshape(bsz, seq, H_C, 2, D_C), k_norm_g), cos, sin)
    cv = c_v.reshape(bsz, seq, H_C, DV_C)
    o_c = diff_attention(cq, ck, cv, lam)
    y_c = (rms_norm(o_c, subln_g).astype(F32).reshape(bsz, seq, W_C) * (1.0 - lambda_init)
           * jax.nn.silu(c_z.astype(F32))).astype(dt)

    gates = jax.nn.sigmoid(gate_logits.reshape(bsz, seq, N_BRANCH, D_MODEL))
    merged = gates[:, :, 0] * (y_a @ w_br_a) + gates[:, :, 1] * (y_b @ w_br_b) + gates[:, :, 2] * (y_c @ w_br_c)
    return (x + merged @ w_out).astype(dt)


def trunk(x, norm_g, w_in, conv_w, a_log, dt_bias, gdn_norm_g, hgrn_lb_logits, hgrn_norm_g,
          q_norm_g, k_norm_g, diff_lambda, subln_g, w_br_a, w_br_b, w_br_c, w_out):
    cos, sin = rope_tables(x.shape[1])
    lbs = hgrn_lower_bounds(hgrn_lb_logits)
    for l in range(DEPTH):
        x = layer(x, l, lbs, cos, sin, norm_g[l], w_in[l], conv_w[l], a_log[l], dt_bias[l], gdn_norm_g[l],
                  hgrn_norm_g[l], q_norm_g[l], k_norm_g[l], diff_lambda[l], subln_g[l],
                  w_br_a[l], w_br_b[l], w_br_c[l], w_out[l])
    return x


def setup_inputs(seed: int = 0) -> dict:
    key = jax.random.key(seed)
    ks = jax.random.split(key, 20)

    def nrm(k, shape, scale):
        return scale * jax.random.normal(k, shape, F32)

    x_prompt = nrm(ks[0], (BATCH, SEQ, D_MODEL), 1.0)
    x_sample = nrm(ks[1], (DEC_BATCH, DEC_SEQ, D_MODEL), 1.0)
    norm_g = 1.0 + nrm(ks[2], (DEPTH, D_MODEL), 0.02)
    w_in = nrm(ks[3], (DEPTH, D_MODEL, N_IN), D_MODEL ** -0.5)
    conv_w = nrm(ks[4], (DEPTH, CONV_K, QKV_A), CONV_K ** -0.5)
    a_log = jnp.log(jax.random.uniform(ks[5], (DEPTH, 2, H_A), F32, 1.0, 16.0))
    dt0 = jnp.exp(jax.random.uniform(ks[6], (DEPTH, 2, H_A), F32, math.log(1e-3), math.log(1e-1)))
    dt_bias = dt0 + jnp.log(-jnp.expm1(-dt0))
    gdn_norm_g = 1.0 + nrm(ks[7], (DEPTH, DV_A), 0.02)
    hgrn_lb_logits = nrm(ks[8], (2, DEPTH, W_B), 0.1)
    hgrn_norm_g = 1.0 + nrm(ks[9], (DEPTH, DV_B), 0.02)
    q_norm_g = 1.0 + nrm(ks[10], (DEPTH, 2, D_C), 0.02)
    k_norm_g = 1.0 + nrm(ks[11], (DEPTH, 2, D_C), 0.02)
    diff_lambda = nrm(ks[12], (DEPTH, 4, D_C), 0.1)
    subln_g = 1.0 + nrm(ks[13], (DEPTH, DV_C), 0.02)
    w_br_a = nrm(ks[14], (DEPTH, W_A, D_MODEL), W_A ** -0.5)
    w_br_b = nrm(ks[15], (DEPTH, W_B, D_MODEL), W_B ** -0.5)
    w_br_c = nrm(ks[16], (DEPTH, W_C, D_MODEL), W_C ** -0.5)
    w_out = nrm(ks[17], (DEPTH, D_MODEL, D_MODEL), D_MODEL ** -0.5)
    return {'x_prompt': x_prompt, 'x_sample': x_sample, 'norm_g': norm_g, 'w_in': w_in, 'conv_w': conv_w,
            'a_log': a_log, 'dt_bias': dt_bias, 'gdn_norm_g': gdn_norm_g, 'hgrn_lb_logits': hgrn_lb_logits,
            'hgrn_norm_g': hgrn_norm_g, 'q_norm_g': q_norm_g, 'k_norm_g': k_norm_g, 'diff_lambda': diff_lambda,
            'subln_g': subln_g, 'w_br_a': w_br_a, 'w_br_b': w_br_b, 'w_br_c': w_br_c, 'w_out': w_out}


def reference(x_prompt, x_sample, norm_g, w_in, conv_w, a_log, dt_bias, gdn_norm_g, hgrn_lb_logits, hgrn_norm_g,
              q_norm_g, k_norm_g, diff_lambda, subln_g, w_br_a, w_br_b, w_br_c, w_out):
    y_prompt = trunk(x_prompt, norm_g, w_in, conv_w, a_log, dt_bias, gdn_norm_g, hgrn_lb_logits, hgrn_norm_g,
                     q_norm_g, k_norm_g, diff_lambda, subln_g, w_br_a, w_br_b, w_br_c, w_out)
    y_sample = trunk(x_sample, norm_g, w_in, conv_w, a_log, dt_bias, gdn_norm_g, hgrn_lb_logits, hgrn_norm_g,
                     q_norm_g, k_norm_g, diff_lambda, subln_g, w_br_a, w_br_b, w_br_c, w_out)
    return (y_prompt, y_sample)
```

```python
import functools
import math

import jax
import jax.numpy as jnp
from jax import lax
from jax.experimental import pallas as pl
from jax.experimental.pallas import tpu as pltpu

F32 = jnp.float32
BF16 = jnp.bfloat16
EPS = 1e-6
F_MIN_GAP = 1e-6

D_MODEL = 1024
LANES = 128
N_HEADS = 4
HEAD_W = 128
CONV_K = 5
CHUNK_A = 64
CHUNK_B = 16
D_C = 64
ROPE_DIM = D_C // 4
ROPE_THETA = 500000.0
N_SMALL = 16
N_MAIN = 9728
CB_AQ, CB_AK, CB_AV, CB_AZ = 24, 28, 32, 36
CB_BQ, CB_BI, CB_BF, CB_BZ = 40, 44, 48, 56
CB_CQ, CB_CK, CB_CV, CB_CZ = 60, 64, 68, 72
VMEM_LIMIT = 56 * 1024 * 1024


def _cparams(sem):
    return pltpu.CompilerParams(dimension_semantics=sem, vmem_limit_bytes=VMEM_LIMIT)


def _inproj_body(x_ref, g_ref, w_ref, ws_ref, o_ref, os_ref, h_ref):
    @pl.when(pl.program_id(1) == 0)
    def _():
        x = x_ref[...]
        h = x * lax.rsqrt(jnp.mean(x * x, axis=-1, keepdims=True) + EPS) * g_ref[...]
        hb = h.astype(BF16)
        h_ref[...] = hb
        os_ref[...] = jnp.dot(hb, ws_ref[...], preferred_element_type=F32)

    o_ref[...] = jnp.dot(h_ref[...], w_ref[...], preferred_element_type=F32).astype(BF16)


def _inproj(x2d, g, w_main, w_small, tm=512, tn=2432):
    t = x2d.shape[0]
    return pl.pallas_call(
        _inproj_body,
        grid=(t // tm, N_MAIN // tn),
        in_specs=[
            pl.BlockSpec((tm, D_MODEL), lambda i, j: (i, 0)),
            pl.BlockSpec((1, D_MODEL), lambda i, j: (0, 0)),
            pl.BlockSpec((D_MODEL, tn), lambda i, j: (0, j)),
            pl.BlockSpec((D_MODEL, LANES), lambda i, j: (0, 0)),
        ],
        out_specs=[
            pl.BlockSpec((tm, tn), lambda i, j: (i, j)),
            pl.BlockSpec((tm, LANES), lambda i, j: (i, 0)),
        ],
        out_shape=[
            jax.ShapeDtypeStruct((t, N_MAIN), BF16),
            jax.ShapeDtypeStruct((t, LANES), F32),
        ],
        scratch_shapes=[pltpu.VMEM((tm, D_MODEL), BF16)],
        compiler_params=_cparams(("parallel", "arbitrary")),
        name="inproj",
    )(x2d, g, w_main, w_small)


def _qkprep_body(q_ref, k_ref, c_ref, s1_ref, s2_ref, qg_ref, kg_ref, qo_ref, ko_ref, *, q_scale):
    lane = lax.broadcasted_iota(jnp.int32, (1, LANES), 1)
    cos, s1, s2 = c_ref[...], s1_ref[...], s2_ref[...]

    def prep(x, g):
        x = x.astype(F32)
        x2 = x * x
        lo = jnp.sum(x2[:, :D_C], axis=-1, keepdims=True)
        hi = jnp.sum(x2[:, D_C:], axis=-1, keepdims=True)
        ms = jnp.where(lane < D_C, lo, hi) * (1.0 / D_C)
        y = x * lax.rsqrt(ms + EPS) * g
        return y * cos + pltpu.roll(y, ROPE_DIM // 2, 1) * s1 + pltpu.roll(y, LANES - ROPE_DIM // 2, 1) * s2

    qo_ref[0] = (prep(q_ref[0], qg_ref[...]) * q_scale).astype(BF16)
    ko_ref[0] = prep(k_ref[0], kg_ref[...]).astype(BF16)


def _qkprep(proj, cos_t, s1_t, s2_t, qg, kg, ts=1024):
    b, s, _ = proj.shape
    ts = min(ts, s)
    q_scale = (D_C ** -0.5) * math.log2(math.e)
    tab = pl.BlockSpec((ts, LANES), lambda bi, si, h: (si, 0))
    vec = pl.BlockSpec((1, LANES), lambda bi, si, h: (0, 0))
    out = pl.BlockSpec((1, ts, LANES), lambda bi, si, h: (bi, si, h))
    return pl.pallas_call(
        functools.partial(_qkprep_body, q_scale=q_scale),
        grid=(b, s // ts, N_HEADS),
        in_specs=[
            pl.BlockSpec((1, ts, LANES), lambda bi, si, h: (bi, si, CB_CQ + h)),
            pl.BlockSpec((1, ts, LANES), lambda bi, si, h: (bi, si, CB_CK + h)),
            tab, tab, tab, vec, vec,
        ],
        out_specs=[out, out],
        out_shape=[jax.ShapeDtypeStruct((b, s, N_HEADS * HEAD_W), BF16)] * 2,
        compiler_params=_cparams(("parallel", "parallel", "parallel")),
        name="qkprep",
    )(proj, proj, cos_t, s1_t, s2_t, qg, kg)


def _attn_body(sc_ref, q_ref, k_ref, v_ref, z_ref, g_ref, o_ref):
    lam = sc_ref[0]
    out_scale = sc_ref[1]
    q = q_ref[0]
    k = k_ref[0]
    v = v_ref[0]
    lane = lax.broadcasted_iota(jnp.int32, (1, LANES), 1)
    zero = jnp.zeros_like(q)

    def one_map(qm):
        s = lax.dot_general(qm, k, (((1,), (1,)), ((), ())), preferred_element_type=F32)
        m = jnp.max(s, axis=-1, keepdims=True)
        p = jnp.exp2(s - m)
        l = jnp.sum(p, axis=-1, keepdims=True)
        return jnp.dot(p.astype(BF16), v, preferred_element_type=F32) / l

    o = one_map(jnp.where(lane < D_C, q, zero)) - lam * one_map(jnp.where(lane >= D_C, q, zero))
    y = o * lax.rsqrt(jnp.mean(o * o, axis=-1, keepdims=True) + EPS) * g_ref[...]
    z = z_ref[0].astype(F32)
    o_ref[0] = (y * out_scale * (z * jax.nn.sigmoid(z))).astype(BF16)


def _attn(scal, qn, kn, proj, subln_g, tq=256):
    b, s, _ = proj.shape
    tq = min(tq, s)
    return pl.pallas_call(
        _attn_body,
        grid=(b, N_HEADS, s // tq),
        in_specs=[
            pl.BlockSpec(memory_space=pltpu.SMEM),
            pl.BlockSpec((1, tq, LANES), lambda bi, h, qi: (bi, qi, h)),
            pl.BlockSpec((1, s, LANES), lambda bi, h, qi: (bi, 0, h)),
            pl.BlockSpec((1, s, LANES), lambda bi, h, qi: (bi, 0, CB_CV + h)),
            pl.BlockSpec((1, tq, LANES), lambda bi, h, qi: (bi, qi, CB_CZ + h)),
            pl.BlockSpec((1, LANES), lambda bi, h, qi: (0, 0)),
        ],
        out_specs=pl.BlockSpec((1, tq, LANES), lambda bi, h, qi: (bi, qi, h)),
        out_shape=jax.ShapeDtypeStruct((b, s, N_HEADS * HEAD_W), BF16),
        compiler_params=_cparams(("parallel", "parallel", "arbitrary")),
        name="diffattn",
    )(scal, qn, kn, proj, proj, subln_g)


def _merge_body(x_ref, ya_ref, yb_ref, yc_ref, g0_ref, g1_ref, g2_ref, wa_ref, wb_ref, wc_ref, wo_ref, o_ref):
    def branch(y_ref, w_ref, g_ref):
        return jax.nn.sigmoid(g_ref[...].astype(F32)) * jnp.dot(y_ref[...], w_ref[...], preferred_element_type=F32)

    merged = branch(ya_ref, wa_ref, g0_ref) + branch(yb_ref, wb_ref, g1_ref) + branch(yc_ref, wc_ref, g2_ref)
    o_ref[...] = x_ref[...] + jnp.dot(merged.astype(BF16), wo_ref[...], preferred_element_type=F32)


def _merge(x2d, ya, yb, yc, proj2d, wa, wb, wc, wo, tm=512):
    t = x2d.shape[0]
    w_br = N_HEADS * HEAD_W
    row = lambda i: (i, 0)
    full = lambda i: (0, 0)
    y_spec = pl.BlockSpec((tm, w_br), row)
    wbr_spec = pl.BlockSpec((w_br, D_MODEL), full)
    return pl.pallas_call(
        _merge_body,
        grid=(t // tm,),
        in_specs=[
            pl.BlockSpec((tm, D_MODEL), row),
            y_spec, y_spec, y_spec,
            pl.BlockSpec((tm, D_MODEL), lambda i: (i, 0)),
            pl.BlockSpec((tm, D_MODEL), lambda i: (i, 1)),
            pl.BlockSpec((tm, D_MODEL), lambda i: (i, 2)),
            wbr_spec, wbr_spec, wbr_spec,
            pl.BlockSpec((D_MODEL, D_MODEL), full),
        ],
        out_specs=pl.BlockSpec((tm, D_MODEL), row),
        out_shape=jax.ShapeDtypeStruct((t, D_MODEL), F32),
        compiler_params=_cparams(("parallel",)),
        name="merge",
    )(x2d, ya, yb, yc, proj2d, proj2d, proj2d, wa, wb, wc, wo)


def _rms(x, g):
    return x * lax.rsqrt(jnp.mean(x * x, axis=-1, keepdims=True) + EPS) * g


def _l2(x):
    return x * lax.rsqrt(jnp.sum(x * x, axis=-1, keepdims=True) + EPS)


def _to_chunks(t, c):
    b, s, h, d = t.shape
    return t.reshape(b, s // c, c, h, d).transpose(1, 0, 3, 2, 4)


def _from_chunks(t):
    n, b, h, c, d = t.shape
    return t.transpose(1, 0, 3, 2, 4).reshape(b, n * c, h, d)


def _mexp(diff, mask):
    return jnp.where(mask, jnp.exp(jnp.where(mask, diff, 0.0)), 0.0)


def _gdn_jax(q, k, v, beta, g):
    bsz, _, nh, dk = k.shape
    dv = v.shape[-1]
    q, k, v = (_to_chunks(t, CHUNK_A) for t in (q, k, v))
    beta = _to_chunks(beta[..., None], CHUNK_A)[..., 0]
    gc = jnp.cumsum(_to_chunks(g[..., None], CHUNK_A)[..., 0], axis=-1)
    incl = jnp.tril(jnp.ones((CHUNK_A, CHUNK_A), bool))
    strict = jnp.tril(jnp.ones((CHUNK_A, CHUNK_A), bool), -1)
    decay = _mexp(gc[..., :, None] - gc[..., None, :], incl)
    kb = k * beta[..., None]
    m = jnp.where(strict, jnp.einsum('nbhid,nbhjd->nbhij', kb, k) * decay, 0.0)
    lhs = m + jnp.eye(CHUNK_A, dtype=F32)
    rhs = jnp.concatenate([v * beta[..., None], kb * jnp.exp(gc)[..., None]], axis=-1)
    sol = lax.linalg.triangular_solve(lhs, rhs, left_side=True, lower=True)
    u, w = sol[..., :dv], sol[..., dv:]
    qs = q * (dk ** -0.5)
    a_qk = jnp.einsum('nbhid,nbhjd->nbhij', qs, k) * decay
    q_dec = qs * jnp.exp(gc)[..., None]
    g_last = gc[..., -1]
    k_dec = k * jnp.exp(g_last[..., None] - gc)[..., None]

    def step(s, inp):
        u_c, w_c, qd_c, a_c, kd_c, gl_c = inp
        v_new = u_c - jnp.einsum('bhck,bhkv->bhcv', w_c, s)
        o = jnp.einsum('bhck,bhkv->bhcv', qd_c, s) + jnp.einsum('bhij,bhjv->bhiv', a_c, v_new)
        s = s * jnp.exp(gl_c)[..., None, None] + jnp.einsum('bhck,bhcv->bhkv', kd_c, v_new)
        return s, o

    s0 = jnp.zeros((bsz, nh, dk, dv), F32)
    _, o = lax.scan(step, s0, (u, w, q_dec, a_qk, k_dec, g_last))
    return _from_chunks(o)


def _hgrn_jax(q, k, v, log_f):
    bsz, _, nh, dk = q.shape
    dv = v.shape[-1]
    q, k, v, log_f = (_to_chunks(t, CHUNK_B) for t in (q, k, v, log_f))
    gc = jnp.cumsum(log_f, axis=-2)
    qs = q * (dk ** -0.5)
    q_dec = qs * jnp.exp(gc)
    g_last = gc[..., -1, :]
    k_dec = k * jnp.exp(g_last[..., None, :] - gc)
    incl = jnp.tril(jnp.ones((CHUNK_B, CHUNK_B), bool))[:, :, None]

    def step(s, inp):
        qs_c, qd_c, k_c, kd_c, v_c, gc_c, gl_c = inp
        pair = _mexp(gc_c[:, :, :, None, :] - gc_c[:, :, None, :, :], incl)
        a = jnp.einsum('bhik,bhjk,bhijk->bhij', qs_c, k_c, pair)
        o = jnp.einsum('bhck,bhkv->bhcv', qd_c, s) + jnp.einsum('bhij,bhjv->bhiv', a, v_c)
        s = s * jnp.exp(gl_c)[..., None] + jnp.einsum('bhck,bhcv->bhkv', kd_c, v_c)
        return s, o

    s0 = jnp.zeros((bsz, nh, dk, dv), F32)
    _, o = lax.scan(step, s0, (qs, q_dec, k, k_dec, v, gc, g_last))
    return _from_chunks(o)


def _flip(t):
    return jnp.flip(t, axis=1)


def _mixer_a_jax(proj, small, conv_w, a_log, dt_bias, gdn_norm_g):
    b, s, _ = proj.shape
    col = lambda cb, n: proj[:, :, cb * LANES:(cb + n) * LANES].astype(F32)
    qkv = col(CB_AQ, 12)
    y = lax.conv_general_dilated(qkv, conv_w[:, None, :], window_strides=(1,),
                                 padding=[((CONV_K - 1) // 2, CONV_K // 2)],
                                 dimension_numbers=('NWC', 'WIO', 'NWC'), feature_group_count=qkv.shape[-1])
    y = jax.nn.silu(y)
    aq = _l2(y[..., :512].reshape(b, s, 4, 128))
    ak = _l2(y[..., 512:1024].reshape(b, s, 4, 128))
    av = y[..., 1024:].reshape(b, s, 4, 128)
    beta = jax.nn.sigmoid(small[..., :8]).reshape(b, s, 2, 4)
    ga = -jnp.exp(a_log) * jax.nn.softplus(small[..., 8:16].reshape(b, s, 2, 4) + dt_bias)
    o = (_gdn_jax(aq, ak, av, beta[:, :, 0], ga[:, :, 0])
         + _flip(_gdn_jax(_flip(aq), _flip(ak), _flip(av), _flip(beta[:, :, 1]), _flip(ga[:, :, 1]))))
    return (_rms(o, gdn_norm_g).reshape(b, s, 512) * jax.nn.silu(col(CB_AZ, 4))).astype(BF16)


def _mixer_b_jax(proj, lb, hgrn_norm_g):
    b, s, _ = proj.shape
    col = lambda cb, n: proj[:, :, cb * LANES:(cb + n) * LANES].astype(F32)
    bq = jax.nn.silu(col(CB_BQ, 4)).reshape(b, s, 4, 128)
    bi = col(CB_BI, 4).reshape(b, s, 4, 128)
    bf = col(CB_BF, 8).reshape(b, s, 2, 4, 128)
    kk = (1.0 - lb.reshape(2, 4, 128)) * jax.nn.sigmoid(-bf)
    log_f = jnp.log1p(-jnp.minimum(kk, 1.0 - F_MIN_GAP))
    o = (_hgrn_jax(bq, kk[:, :, 0], bi, log_f[:, :, 0])
         + _flip(_hgrn_jax(_flip(bq), _flip(kk[:, :, 1]), _flip(bi), _flip(log_f[:, :, 1]))))
    return (_rms(o, hgrn_norm_g).reshape(b, s, 512) * jax.nn.silu(col(CB_BZ, 4))).astype(BF16)


def _rope_tables(seq):
    half = ROPE_DIM // 2
    inv = 1.0 / (ROPE_THETA ** (jnp.arange(0, ROPE_DIM, 2, dtype=F32) / ROPE_DIM))
    ang = jnp.arange(seq, dtype=F32)[:, None] * inv[None, :]
    cos, sin = jnp.cos(ang), jnp.sin(ang)
    one = jnp.ones((seq, D_C - ROPE_DIM), F32)
    zero = jnp.zeros((seq, D_C - ROPE_DIM), F32)
    zh = jnp.zeros((seq, half), F32)
    c_map = jnp.concatenate([cos, cos, one], axis=1)
    s1_map = jnp.concatenate([zh, sin, zero], axis=1)
    s2_map = jnp.concatenate([-sin, zh, zero], axis=1)
    tile2 = lambda t: jnp.concatenate([t, t], axis=1)
    return tile2(c_map), tile2(s1_map), tile2(s2_map)


def _lower_bounds(lb_logits):
    p = jax.nn.softmax(lb_logits.astype(F32), axis=1)
    return jnp.cumsum(p, axis=1) - p[:, :1]


def kernel(x_prompt, x_sample, norm_g, w_in, conv_w, a_log, dt_bias, gdn_norm_g, hgrn_lb_logits, hgrn_norm_g,
           q_norm_g, k_norm_g, diff_lambda, subln_g, w_br_a, w_br_b, w_br_c, w_out):
    nb_p = x_prompt.shape[0]
    x = jnp.concatenate([x_prompt, x_sample], axis=0)
    b, s, d = x.shape
    depth = w_in.shape[0]
    n_in = w_in.shape[-1]
    gate0 = n_in - 3 * D_MODEL
    w_main = jnp.concatenate([w_in[:, :, gate0:], w_in[:, :, :2048], w_in[:, :, 2048 + N_SMALL:gate0]],
                             axis=-1).astype(BF16)
    w_small = jnp.pad(w_in[:, :, 2048:2048 + N_SMALL], ((0, 0), (0, 0), (0, LANES - N_SMALL))).astype(BF16)
    wa, wb, wc, wo = (w.astype(BF16) for w in (w_br_a, w_br_b, w_br_c, w_out))
    cos_t, s1_t, s2_t = _rope_tables(s)
    lbs = _lower_bounds(hgrn_lb_logits)
    lp = diff_lambda.astype(F32)
    lam_dyn = jnp.exp(jnp.sum(lp[:, 0] * lp[:, 1], axis=-1)) - jnp.exp(jnp.sum(lp[:, 2] * lp[:, 3], axis=-1))

    x2d = x.reshape(b * s, d)
    for l in range(depth):
        lambda_init = 0.8 - 0.6 * math.exp(-0.3 * l)
        proj2d, small2d = _inproj(x2d, norm_g[l][None, :], w_main[l], w_small[l])
        proj = proj2d.reshape(b, s, N_MAIN)
        small = small2d.reshape(b, s, LANES)
        ya = _mixer_a_jax(proj, small, conv_w[l], a_log[l], dt_bias[l], gdn_norm_g[l])
        yb = _mixer_b_jax(proj, lbs[:, l], hgrn_norm_g[l])
        qn, kn = _qkprep(proj, cos_t, s1_t, s2_t, q_norm_g[l].reshape(1, LANES), k_norm_g[l].reshape(1, LANES))
        scal = jnp.stack([lam_dyn[l] + lambda_init, jnp.asarray(1.0 - lambda_init, F32)]).astype(F32)
        yc = _attn(scal, qn, kn, proj, subln_g[l][None, :])
        x2d = _merge(x2d, ya.reshape(b * s, -1), yb.reshape(b * s, -1), yc.reshape(b * s, -1), proj2d,
                     wa[l], wb[l], wc[l], wo[l])
    y = x2d.reshape(b, s, d)
    return (y[:nb_p], y[nb_p:])
```

```python
import functools
import math

import jax
import jax.numpy as jnp
from jax import lax
from jax.experimental import pallas as pl
from jax.experimental.pallas import tpu as pltpu

F32 = jnp.float32
BF16 = jnp.bfloat16
EPS = 1e-6
F_MIN_GAP = 1e-6

D_MODEL = 1024
LANES = 128
N_HEADS = 4
HEAD_W = 128
CONV_K = 5
CHUNK_A = 64
CHUNK_B = 16
D_C = 64
ROPE_DIM = D_C // 4
ROPE_THETA = 500000.0
N_SMALL = 16
N_MAIN = 9728
CB_AQ, CB_AK, CB_AV, CB_AZ = 24, 28, 32, 36
CB_BQ, CB_BI, CB_BF, CB_BZ = 40, 44, 48, 56
CB_CQ, CB_CK, CB_CV, CB_CZ = 60, 64, 68, 72
VMEM_LIMIT = 56 * 1024 * 1024


def _cparams(sem):
    return pltpu.CompilerParams(dimension_semantics=sem, vmem_limit_bytes=VMEM_LIMIT)


def _inproj_body(x_ref, g_ref, w_ref, ws_ref, o_ref, os_ref, h_ref):
    @pl.when(pl.program_id(1) == 0)
    def _():
        x = x_ref[...]
        h = x * lax.rsqrt(jnp.mean(x * x, axis=-1, keepdims=True) + EPS) * g_ref[...]
        hb = h.astype(BF16)
        h_ref[...] = hb
        os_ref[...] = jnp.dot(hb, ws_ref[...], preferred_element_type=F32)

    o_ref[...] = jnp.dot(h_ref[...], w_ref[...], preferred_element_type=F32).astype(BF16)


def _inproj(x2d, g, w_main, w_small, tm=512, tn=2432):
    t = x2d.shape[0]
    return pl.pallas_call(
        _inproj_body,
        grid=(t // tm, N_MAIN // tn),
        in_specs=[
            pl.BlockSpec((tm, D_MODEL), lambda i, j: (i, 0)),
            pl.BlockSpec((1, D_MODEL), lambda i, j: (0, 0)),
            pl.BlockSpec((D_MODEL, tn), lambda i, j: (0, j)),
            pl.BlockSpec((D_MODEL, LANES), lambda i, j: (0, 0)),
        ],
        out_specs=[
            pl.BlockSpec((tm, tn), lambda i, j: (i, j)),
            pl.BlockSpec((tm, LANES), lambda i, j: (i, 0)),
        ],
        out_shape=[
            jax.ShapeDtypeStruct((t, N_MAIN), BF16),
            jax.ShapeDtypeStruct((t, LANES), F32),
        ],
        scratch_shapes=[pltpu.VMEM((tm, D_MODEL), BF16)],
        compiler_params=_cparams(("parallel", "arbitrary")),
        name="inproj",
    )(x2d, g, w_main, w_small)


def _qkprep_body(q_ref, k_ref, c_ref, s1_ref, s2_ref, qg_ref, kg_ref, qo_ref, ko_ref, *, q_scale):
    lane = lax.broadcasted_iota(jnp.int32, (1, LANES), 1)
    cos, s1, s2 = c_ref[...], s1_ref[...], s2_ref[...]

    def prep(x, g):
        x = x.astype(F32)
        x2 = x * x
        lo = jnp.sum(x2[:, :D_C], axis=-1, keepdims=True)
        hi = jnp.sum(x2[:, D_C:], axis=-1, keepdims=True)
        ms = jnp.where(lane < D_C, lo, hi) * (1.0 / D_C)
        y = x * lax.rsqrt(ms + EPS) * g
        return y * cos + pltpu.roll(y, ROPE_DIM // 2, 1) * s1 + pltpu.roll(y, LANES - ROPE_DIM // 2, 1) * s2

    qo_ref[0] = (prep(q_ref[0], qg_ref[...]) * q_scale).astype(BF16)
    ko_ref[0] = prep(k_ref[0], kg_ref[...]).astype(BF16)


def _qkprep(proj, cos_t, s1_t, s2_t, qg, kg, ts=1024):
    b, s, _ = proj.shape
    ts = min(ts, s)
    q_scale = (D_C ** -0.5) * math.log2(math.e)
    tab = pl.BlockSpec((ts, LANES), lambda bi, si, h: (si, 0))
    vec = pl.BlockSpec((1, LANES), lambda bi, si, h: (0, 0))
    out = pl.BlockSpec((1, ts, LANES), lambda bi, si, h: (bi, si, h))
    return pl.pallas_call(
        functools.partial(_qkprep_body, q_scale=q_scale),
        grid=(b, s // ts, N_HEADS),
        in_specs=[
            pl.BlockSpec((1, ts, LANES), lambda bi, si, h: (bi, si, CB_CQ + h)),
            pl.BlockSpec((1, ts, LANES), lambda bi, si, h: (bi, si, CB_CK + h)),
            tab, tab, tab, vec, vec,
        ],
        out_specs=[out, out],
        out_shape=[jax.ShapeDtypeStruct((b, s, N_HEADS * HEAD_W), BF16)] * 2,
        compiler_params=_cparams(("parallel", "parallel", "parallel")),
        name="qkprep",
    )(proj, proj, cos_t, s1_t, s2_t, qg, kg)


def _attn_body(sc_ref, q_ref, k_ref, v_ref, z_ref, g_ref, o_ref):
    lam = sc_ref[0]
    out_scale = sc_ref[1]
    q = q_ref[0]
    k = k_ref[0]
    v = v_ref[0]
    lane = lax.broadcasted_iota(jnp.int32, (1, LANES), 1)
    zero = jnp.zeros_like(q)

    def one_map(qm):
        s = lax.dot_general(qm, k, (((1,), (1,)), ((), ())), preferred_element_type=F32)
        m = jnp.max(s, axis=-1, keepdims=True)
        p = jnp.exp2(s - m)
        l = jnp.sum(p, axis=-1, keepdims=True)
        return jnp.dot(p.astype(BF16), v, preferred_element_type=F32) / l

    o = one_map(jnp.where(lane < D_C, q, zero)) - lam * one_map(jnp.where(lane >= D_C, q, zero))
    y = o * lax.rsqrt(jnp.mean(o * o, axis=-1, keepdims=True) + EPS) * g_ref[...]
    z = z_ref[0].astype(F32)
    o_ref[0] = (y * out_scale * (z * jax.nn.sigmoid(z))).astype(BF16)


def _attn(scal, qn, kn, proj, subln_g, tq=256):
    b, s, _ = proj.shape
    tq = min(tq, s)
    return pl.pallas_call(
        _attn_body,
        grid=(b, N_HEADS, s // tq),
        in_specs=[
            pl.BlockSpec(memory_space=pltpu.SMEM),
            pl.BlockSpec((1, tq, LANES), lambda bi, h, qi: (bi, qi, h)),
            pl.BlockSpec((1, s, LANES), lambda bi, h, qi: (bi, 0, h)),
            pl.BlockSpec((1, s, LANES), lambda bi, h, qi: (bi, 0, CB_CV + h)),
            pl.BlockSpec((1, tq, LANES), lambda bi, h, qi: (bi, qi, CB_CZ + h)),
            pl.BlockSpec((1, LANES), lambda bi, h, qi: (0, 0)),
        ],
        out_specs=pl.BlockSpec((1, tq, LANES), lambda bi, h, qi: (bi, qi, h)),
        out_shape=jax.ShapeDtypeStruct((b, s, N_HEADS * HEAD_W), BF16),
        compiler_params=_cparams(("parallel", "parallel", "arbitrary")),
        name="diffattn",
    )(scal, qn, kn, proj, proj, subln_g)


def _merge_body(x_ref, ya_ref, yb_ref, yc_ref, g0_ref, g1_ref, g2_ref, wa_ref, wb_ref, wc_ref, wo_ref, o_ref):
    def branch(y_ref, w_ref, g_ref):
        return jax.nn.sigmoid(g_ref[...].astype(F32)) * jnp.dot(y_ref[...], w_ref[...], preferred_element_type=F32)

    merged = branch(ya_ref, wa_ref, g0_ref) + branch(yb_ref, wb_ref, g1_ref) + branch(yc_ref, wc_ref, g2_ref)
    o_ref[...] = x_ref[...] + jnp.dot(merged.astype(BF16), wo_ref[...], preferred_element_type=F32)


def _merge(x2d, ya, yb, yc, proj2d, wa, wb, wc, wo, tm=512):
    t = x2d.shape[0]
    w_br = N_HEADS * HEAD_W
    row = lambda i: (i, 0)
    full = lambda i: (0, 0)
    y_spec = pl.BlockSpec((tm, w_br), row)
    wbr_spec = pl.BlockSpec((w_br, D_MODEL), full)
    return pl.pallas_call(
        _merge_body,
        grid=(t // tm,),
        in_specs=[
            pl.BlockSpec((tm, D_MODEL), row),
            y_spec, y_spec, y_spec,
            pl.BlockSpec((tm, D_MODEL), lambda i: (i, 0)),
            pl.BlockSpec((tm, D_MODEL), lambda i: (i, 1)),
            pl.BlockSpec((tm, D_MODEL), lambda i: (i, 2)),
            wbr_spec, wbr_spec, wbr_spec,
            pl.BlockSpec((D_MODEL, D_MODEL), full),
        ],
        out_specs=pl.BlockSpec((tm, D_MODEL), row),
        out_shape=jax.ShapeDtypeStruct((t, D_MODEL), F32),
        compiler_params=_cparams(("parallel",)),
        name="merge",
    )(x2d, ya, yb, yc, proj2d, proj2d, proj2d, wa, wb, wc, wo)


def _split3(x):
    p1 = x.astype(BF16)
    r1 = x - p1.astype(F32)
    p2 = r1.astype(BF16)
    p3 = (r1 - p2.astype(F32)).astype(BF16)
    return p1, p2, p3


def _softplus(x):
    return jnp.maximum(x, 0.0) + jnp.log1p(jnp.exp(-jnp.abs(x)))


def _silu(x):
    return x * jax.nn.sigmoid(x)


def _dot(a, b):
    return jnp.dot(a, b, preferred_element_type=F32)


def _dot_nt(a, b):
    return lax.dot_general(a, b, (((1,), (1,)), ((), ())), preferred_element_type=F32)


CUM_TILE = 256


def _gdn_body(alog_ref, dtb_ref, q_ref, k_ref, v_ref, z_ref, cwq_ref, cwk_ref, cwv_ref, sm_ref, smt_ref,
              nav_ref, dtv_ref, ng_ref, o_ref,
              xp_ref, qn_ref, kn_ref, vn_ref, gcc_ref, bc_ref, gcr_ref, br_ref,
              u_ref, w_ref, a_ref, qd_ref, kdt_ref, gl_ref, of_ref, ob_ref):
    h = pl.program_id(1)
    seq = q_ref.shape[1]
    c = CHUNK_A
    n_chunks = seq // c
    dk = HEAD_W

    row_t = lax.broadcasted_iota(jnp.int32, (CUM_TILE, CUM_TILE), 0)
    col_t = lax.broadcasted_iota(jnp.int32, (CUM_TILE, CUM_TILE), 1)
    same_chunk = (row_t // c) == (col_t // c)
    tri = (jnp.where(same_chunk & (col_t <= row_t), 1.0, 0.0).astype(BF16),
           jnp.where(same_chunk & (col_t >= row_t), 1.0, 0.0).astype(BF16))
    sel_r = lax.broadcasted_iota(jnp.int32, (LANES, LANES), 0)

    def gates_tile(t, carry):
        rows = pl.ds(pl.multiple_of(t * CUM_TILE, CUM_TILE), CUM_TILE)
        sm = sm_ref[0, rows, :]
        g_parts = _split3(nav_ref[...] * _softplus(sm + dtv_ref[...]))
        b_parts = _split3(jax.nn.sigmoid(sm))
        for d in range(2):
            sel_g = jnp.where(sel_r == 8 + 4 * d + h, 1.0, 0.0).astype(BF16)
            sel_b = jnp.where(sel_r == 4 * d + h, 1.0, 0.0).astype(BF16)
            gc = jnp.zeros((CUM_TILE, LANES), F32)
            bt = jnp.zeros((CUM_TILE, LANES), F32)
            for p in range(3):
                gc = gc + _dot(tri[d], _dot(g_parts[p], sel_g).astype(BF16))
                bt = bt + _dot(b_parts[p], sel_b)
            gcc_ref[d, rows, :] = gc
            bc_ref[d, rows, :] = bt
        return carry

    lax.fori_loop(0, seq // CUM_TILE, gates_tile, 0)

    rc = lax.broadcasted_iota(jnp.int32, (c, c), 0)
    cc = lax.broadcasted_iota(jnp.int32, (c, c), 1)
    for d in range(2):
        a_neg = -jnp.exp(alog_ref[d, h])
        g = a_neg * _softplus(smt_ref[0, 8 + 4 * d + h] + dtb_ref[d, h])
        cum = jnp.where((rc <= cc) if d == 0 else (rc >= cc), 1.0, 0.0).astype(BF16)
        acc = jnp.zeros((n_chunks, c), F32)
        for part in _split3(g):
            acc = acc + _dot(part, cum)
        gcr_ref[d] = acc
        br_ref[d] = jax.nn.sigmoid(smt_ref[0, 4 * d + h])

    pad = 8
    zeros_pad = jnp.zeros((pad, LANES), F32)

    def conv_into(src_ref, cw_ref, dst_ref, normalise, scale):
        xp_ref[pl.ds(0, pad), :] = zeros_pad
        xp_ref[pl.ds(pad + seq, pad), :] = zeros_pad
        xp_ref[pl.ds(pad, seq), :] = src_ref[0].astype(F32)
        cw = cw_ref[...]

        def chunk(n, carry):
            base = pl.multiple_of(n * c, c)
            win = xp_ref[pl.ds(base, c + 2 * pad), :]
            y = jnp.zeros((c, LANES), F32)
            for j in range(CONV_K):
                off = pad - (CONV_K - 1) // 2 + j
                y = y + win[off:off + c, :] * cw[j:j + 1, :]
            y = _silu(y)
            if normalise:
                y = y * lax.rsqrt(jnp.sum(y * y, axis=-1, keepdims=True) + EPS)
            dst_ref[pl.ds(base, c), :] = y * scale if scale != 1.0 else y
            return carry

        lax.fori_loop(0, n_chunks, chunk, 0)

    conv_into(q_ref, cwq_ref, qn_ref, True, dk ** -0.5)
    conv_into(k_ref, cwk_ref, kn_ref, True, 1.0)
    conv_into(v_ref, cwv_ref, vn_ref, False, 1.0)

    eye = jnp.where(rc == cc, 1.0, 0.0)
    for d in range(2):
        incl = (rc >= cc) if d == 0 else (rc <= cc)
        strict = (rc > cc) if d == 0 else (rc < cc)

        def prep_chunk(n, carry, d=d, incl=incl, strict=strict):
            base = pl.multiple_of(n * c, c)
            rows = pl.ds(base, c)
            q = qn_ref[rows, :]
            k = kn_ref[rows, :]
            v = vn_ref[rows, :]
            gcol = gcc_ref[d, rows, :]
            gcol_c = gcol[:, :c]
            grow = gcr_ref[d, pl.ds(n, 1), :]
            brow = br_ref[d, pl.ds(n, 1), :]
            bcol_c = bc_ref[d, rows, :][:, :c]
            decay = jnp.where(incl, jnp.exp(jnp.where(incl, gcol_c - grow, 0.0)), 0.0)
            k16 = k.astype(BF16)
            kk = _dot_nt(k16, k16)
            qk = _dot_nt(q.astype(BF16), k16)
            m = jnp.where(strict, kk * decay * bcol_c, 0.0)
            x = -m
            t_inv = eye + x
            pw = x
            for _ in range(5):
                pw16 = pw.astype(BF16)
                pw = _dot(pw16, pw16)
                t_inv = t_inv + _dot(t_inv.astype(BF16), pw.astype(BF16))
            tb = t_inv * brow
            u_ref[d, rows, :] = _dot(tb.astype(BF16), v.astype(BF16))
            w_ref[d, rows, :] = _dot((tb * jnp.exp(grow)).astype(BF16), k16).astype(BF16)
            a_ref[d, rows, :] = (qk * decay).astype(BF16)
            qd_ref[d, rows, :] = (q * jnp.exp(gcol)).astype(BF16)
            last = (c - 1) if d == 0 else 0
            g_last = gcol[last:last + 1, :]
            kdt_ref[d, n] = (k * jnp.exp(g_last - gcol)).T.astype(BF16)
            gl_ref[d, pl.ds(n, 1), :] = jnp.exp(g_last)
            return carry

        lax.fori_loop(0, n_chunks, prep_chunk, 0, unroll=4)

    def scan_step(i, carry):
        s_f, s_b = carry
        outs = []
        for d, s, o_out in ((0, s_f, of_ref), (1, s_b, ob_ref)):
            n = i if d == 0 else n_chunks - 1 - i
            rows = pl.ds(pl.multiple_of(n * c, c), c)
            s16 = s.astype(BF16)
            v_new = u_ref[d, rows, :] - _dot(w_ref[d, rows, :], s16)
            v16 = v_new.astype(BF16)
            o_out[rows, :] = _dot(qd_ref[d, rows, :], s16) + _dot(a_ref[d, rows, :], v16)
            outs.append(s * gl_ref[d, pl.ds(n, 1), :] + _dot(kdt_ref[d, n], v16))
        return tuple(outs)

    s0 = jnp.zeros((dk, HEAD_W), F32)
    lax.fori_loop(0, n_chunks, scan_step, (s0, s0))

    def finish(t, carry):
        rows = pl.ds(pl.multiple_of(t * CUM_TILE, CUM_TILE), CUM_TILE)
        o = of_ref[rows, :] + ob_ref[rows, :]
        y = o * lax.rsqrt(jnp.mean(o * o, axis=-1, keepdims=True) + EPS) * ng_ref[...]
        o_ref[0, rows, :] = (y * _silu(z_ref[0, rows, :].astype(F32))).astype(BF16)
        return carry

    lax.fori_loop(0, seq // CUM_TILE, finish, 0)


def _mixer_a(proj, small, small_t, conv_w, a_log, dt_bias, neg_a_vec, dtb_vec, norm_g):
    b, s, _ = proj.shape
    n_chunks = s // CHUNK_A
    col = lambda cb: pl.BlockSpec((1, s, LANES), lambda bi, h: (bi, 0, cb + h))
    cw = lambda cb: pl.BlockSpec((CONV_K, LANES), lambda bi, h: (0, cb + h))
    vec = pl.BlockSpec((1, LANES), lambda bi, h: (0, 0))
    smem = pl.BlockSpec(memory_space=pltpu.SMEM)
    return pl.pallas_call(
        _gdn_body,
        grid=(b, N_HEADS),
        in_specs=[
            smem, smem,
            col(CB_AQ), col(CB_AK), col(CB_AV), col(CB_AZ),
            cw(0), cw(4), cw(8),
            pl.BlockSpec((1, s, LANES), lambda bi, h: (bi, 0, 0)),
            pl.BlockSpec((1, N_SMALL, n_chunks, CHUNK_A), lambda bi, h: (bi, 0, 0, 0)),
            vec, vec, vec,
        ],
        out_specs=pl.BlockSpec((1, s, LANES), lambda bi, h: (bi, 0, h)),
        out_shape=jax.ShapeDtypeStruct((b, s, N_HEADS * HEAD_W), BF16),
        scratch_shapes=[
            pltpu.VMEM((s + 16, LANES), F32),
            pltpu.VMEM((s, LANES), F32),
            pltpu.VMEM((s, LANES), F32),
            pltpu.VMEM((s, LANES), F32),
            pltpu.VMEM((2, s, LANES), F32),
            pltpu.VMEM((2, s, LANES), F32),
            pltpu.VMEM((2, n_chunks, CHUNK_A), F32),
            pltpu.VMEM((2, n_chunks, CHUNK_A), F32),
            pltpu.VMEM((2, s, LANES), F32),
            pltpu.VMEM((2, s, LANES), BF16),
            pltpu.VMEM((2, s, CHUNK_A), BF16),
            pltpu.VMEM((2, s, LANES), BF16),
            pltpu.VMEM((2, n_chunks, HEAD_W, CHUNK_A), BF16),
            pltpu.VMEM((2, n_chunks, LANES), F32),
            pltpu.VMEM((s, LANES), F32),
            pltpu.VMEM((s, LANES), F32),
        ],
        compiler_params=_cparams(("parallel", "arbitrary")),
        name="gdn",
    )(a_log, dt_bias, proj, proj, proj, proj, conv_w, conv_w, conv_w, small, small_t, neg_a_vec, dtb_vec, norm_g)


HGRN_GROUP = 4


def _hgrn_body(q_ref, i_ref, ff_ref, fb_ref, z_ref, lb_ref, ng_ref, o_ref,
               qs_ref, kk_ref, gc_ref, gt_ref, qd_ref, kd_ref, of_ref, ob_ref):
    seq = q_ref.shape[1]
    c = CHUNK_B
    n_chunks = seq // c
    dk = HEAD_W

    row_t = lax.broadcasted_iota(jnp.int32, (CUM_TILE, CUM_TILE), 0)
    col_t = lax.broadcasted_iota(jnp.int32, (CUM_TILE, CUM_TILE), 1)
    same_chunk = (row_t // c) == (col_t // c)
    tri = (jnp.where(same_chunk & (col_t <= row_t), 1.0, 0.0).astype(BF16),
           jnp.where(same_chunk & (col_t >= row_t), 1.0, 0.0).astype(BF16))
    tot = jnp.where(same_chunk, 1.0, 0.0).astype(BF16)

    def gates_tile(t, carry):
        rows = pl.ds(pl.multiple_of(t * CUM_TILE, CUM_TILE), CUM_TILE)
        qs = _silu(q_ref[0, rows, :].astype(F32)) * (dk ** -0.5)
        qs_ref[rows, :] = qs
        for d, f_ref in ((0, ff_ref), (1, fb_ref)):
            bf = f_ref[0, rows, :].astype(F32)
            kk = (1.0 - lb_ref[d:d + 1, :]) * jax.nn.sigmoid(-bf)
            log_f = jnp.log1p(-jnp.minimum(kk, 1.0 - F_MIN_GAP))
            gc = jnp.zeros((CUM_TILE, LANES), F32)
            gt = jnp.zeros((CUM_TILE, LANES), F32)
            for part in _split3(log_f):
                gc = gc + _dot(tri[d], part)
                gt = gt + _dot(tot, part)
            kk_ref[d, rows, :] = kk
            gc_ref[d, rows, :] = gc
            gt_ref[d, rows, :] = gt
            qd_ref[d, rows, :] = (qs * jnp.exp(gc)).astype(BF16)
            kd_ref[d, rows, :] = (kk * jnp.exp(gt - gc)).astype(BF16)
        return carry

    lax.fori_loop(0, seq // CUM_TILE, gates_tile, 0)

    jrow = lax.broadcasted_iota(jnp.int32, (c, LANES), 0)
    ones_w = jnp.ones((LANES, LANES), BF16)
    sel = jnp.where(lax.broadcasted_iota(jnp.int32, (c, c * c), 1) // c
                    == lax.broadcasted_iota(jnp.int32, (c, c * c), 0), 1.0, 0.0).astype(BF16)

    def one_chunk(d, n, st, o_out):
        rows = pl.ds(pl.multiple_of(n * c, c), c)
        qs = qs_ref[rows, :]
        kk = kk_ref[d, rows, :]
        gc = gc_ref[d, rows, :]
        v16 = i_ref[0, rows, :]
        v = v16.astype(F32)
        tiles = []
        for i in range(c):
            mask = (jrow <= i) if d == 0 else (jrow >= i)
            pair = jnp.where(mask, jnp.exp(jnp.where(mask, gc[i:i + 1, :] - gc, 0.0)), 0.0)
            tiles.append((pair * kk * qs[i:i + 1, :]).astype(BF16))
        a_rep = _dot(jnp.concatenate(tiles, axis=0), ones_w)
        p = (a_rep * jnp.concatenate([v] * c, axis=0)).astype(BF16)
        o_intra = _dot(sel, p)
        st16 = st.astype(BF16)
        o_out[rows, :] = o_intra + _dot_nt(qd_ref[d, rows, :], st16)
        e_row = jnp.exp(gt_ref[d, pl.ds(pl.multiple_of(n * c, c), 1), :])
        upd = lax.dot_general(v16, kd_ref[d, rows, :], (((0,), (0,)), ((), ())), preferred_element_type=F32)
        return st * e_row + upd

    def scan_step(i, carry):
        st_f, st_b = carry
        for g in range(HGRN_GROUP):
            n = i * HGRN_GROUP + g
            st_f = one_chunk(0, n, st_f, of_ref)
            st_b = one_chunk(1, n_chunks - 1 - n, st_b, ob_ref)
        return st_f, st_b

    s0 = jnp.zeros((HEAD_W, dk), F32)
    lax.fori_loop(0, n_chunks // HGRN_GROUP, scan_step, (s0, s0))

    def finish(t, carry):
        rows = pl.ds(pl.multiple_of(t * CUM_TILE, CUM_TILE), CUM_TILE)
        o = of_ref[rows, :] + ob_ref[rows, :]
        y = o * lax.rsqrt(jnp.mean(o * o, axis=-1, keepdims=True) + EPS) * ng_ref[...]
        o_ref[0, rows, :] = (y * _silu(z_ref[0, rows, :].astype(F32))).astype(BF16)
        return carry

    lax.fori_loop(0, seq // CUM_TILE, finish, 0)


def _mixer_b(proj, lb, norm_g):
    b, s, _ = proj.shape
    col = lambda cb: pl.BlockSpec((1, s, LANES), lambda bi, h: (bi, 0, cb + h))
    return pl.pallas_call(
        _hgrn_body,
        grid=(b, N_HEADS),
        in_specs=[
            col(CB_BQ), col(CB_BI), col(CB_BF), col(CB_BF + N_HEADS), col(CB_BZ),
            pl.BlockSpec((2, LANES), lambda bi, h: (0, h)),
            pl.BlockSpec((1, LANES), lambda bi, h: (0, 0)),
        ],
        out_specs=pl.BlockSpec((1, s, LANES), lambda bi, h: (bi, 0, h)),
        out_shape=jax.ShapeDtypeStruct((b, s, N_HEADS * HEAD_W), BF16),
        scratch_shapes=[
            pltpu.VMEM((s, LANES), F32),
            pltpu.VMEM((2, s, LANES), F32),
            pltpu.VMEM((2, s, LANES), F32),
            pltpu.VMEM((2, s, LANES), F32),
            pltpu.VMEM((2, s, LANES), BF16),
            pltpu.VMEM((2, s, LANES), BF16),
            pltpu.VMEM((s, LANES), F32),
            pltpu.VMEM((s, LANES), F32),
        ],
        compiler_params=_cparams(("parallel", "arbitrary")),
        name="hgrn",
    )(proj, proj, proj, proj, proj, lb, norm_g)


def _rms(x, g):
    return x * lax.rsqrt(jnp.mean(x * x, axis=-1, keepdims=True) + EPS) * g


def _l2(x):
    return x * lax.rsqrt(jnp.sum(x * x, axis=-1, keepdims=True) + EPS)


def _to_chunks(t, c):
    b, s, h, d = t.shape
    return t.reshape(b, s // c, c, h, d).transpose(1, 0, 3, 2, 4)


def _from_chunks(t):
    n, b, h, c, d = t.shape
    return t.transpose(1, 0, 3, 2, 4).reshape(b, n * c, h, d)


def _mexp(diff, mask):
    return jnp.where(mask, jnp.exp(jnp.where(mask, diff, 0.0)), 0.0)


def _gdn_jax(q, k, v, beta, g):
    bsz, _, nh, dk = k.shape
    dv = v.shape[-1]
    q, k, v = (_to_chunks(t, CHUNK_A) for t in (q, k, v))
    beta = _to_chunks(beta[..., None], CHUNK_A)[..., 0]
    gc = jnp.cumsum(_to_chunks(g[..., None], CHUNK_A)[..., 0], axis=-1)
    incl = jnp.tril(jnp.ones((CHUNK_A, CHUNK_A), bool))
    strict = jnp.tril(jnp.ones((CHUNK_A, CHUNK_A), bool), -1)
    decay = _mexp(gc[..., :, None] - gc[..., None, :], incl)
    kb = k * beta[..., None]
    m = jnp.where(strict, jnp.einsum('nbhid,nbhjd->nbhij', kb, k) * decay, 0.0)
    lhs = m + jnp.eye(CHUNK_A, dtype=F32)
    rhs = jnp.concatenate([v * beta[..., None], kb * jnp.exp(gc)[..., None]], axis=-1)
    sol = lax.linalg.triangular_solve(lhs, rhs, left_side=True, lower=True)
    u, w = sol[..., :dv], sol[..., dv:]
    qs = q * (dk ** -0.5)
    a_qk = jnp.einsum('nbhid,nbhjd->nbhij', qs, k) * decay
    q_dec = qs * jnp.exp(gc)[..., None]
    g_last = gc[..., -1]
    k_dec = k * jnp.exp(g_last[..., None] - gc)[..., None]

    def step(s, inp):
        u_c, w_c, qd_c, a_c, kd_c, gl_c = inp
        v_new = u_c - jnp.einsum('bhck,bhkv->bhcv', w_c, s)
        o = jnp.einsum('bhck,bhkv->bhcv', qd_c, s) + jnp.einsum('bhij,bhjv->bhiv', a_c, v_new)
        s = s * jnp.exp(gl_c)[..., None, None] + jnp.einsum('bhck,bhcv->bhkv', kd_c, v_new)
        return s, o

    s0 = jnp.zeros((bsz, nh, dk, dv), F32)
    _, o = lax.scan(step, s0, (u, w, q_dec, a_qk, k_dec, g_last))
    return _from_chunks(o)


def _hgrn_jax(q, k, v, log_f):
    bsz, _, nh, dk = q.shape
    dv = v.shape[-1]
    q, k, v, log_f = (_to_chunks(t, CHUNK_B) for t in (q, k, v, log_f))
    gc = jnp.cumsum(log_f, axis=-2)
    qs = q * (dk ** -0.5)
    q_dec = qs * jnp.exp(gc)
    g_last = gc[..., -1, :]
    k_dec = k * jnp.exp(g_last[..., None, :] - gc)
    incl = jnp.tril(jnp.ones((CHUNK_B, CHUNK_B), bool))[:, :, None]

    def step(s, inp):
        qs_c, qd_c, k_c, kd_c, v_c, gc_c, gl_c = inp
        pair = _mexp(gc_c[:, :, :, None, :] - gc_c[:, :, None, :, :], incl)
        a = jnp.einsum('bhik,bhjk,bhijk->bhij', qs_c, k_c, pair)
        o = jnp.einsum('bhck,bhkv->bhcv', qd_c, s) + jnp.einsum('bhij,bhjv->bhiv', a, v_c)
        s = s * jnp.exp(gl_c)[..., None] + jnp.einsum('bhck,bhcv->bhkv', kd_c, v_c)
        return s, o

    s0 = jnp.zeros((bsz, nh, dk, dv), F32)
    _, o = lax.scan(step, s0, (qs, q_dec, k, k_dec, v, gc, g_last))
    return _from_chunks(o)


def _flip(t):
    return jnp.flip(t, axis=1)


def _mixer_a_jax(proj, small, conv_w, a_log, dt_bias, gdn_norm_g):
    b, s, _ = proj.shape
    col = lambda cb, n: proj[:, :, cb * LANES:(cb + n) * LANES].astype(F32)
    qkv = col(CB_AQ, 12)
    y = lax.conv_general_dilated(qkv, conv_w[:, None, :], window_strides=(1,),
                                 padding=[((CONV_K - 1) // 2, CONV_K // 2)],
                                 dimension_numbers=('NWC', 'WIO', 'NWC'), feature_group_count=qkv.shape[-1])
    y = jax.nn.silu(y)
    aq = _l2(y[..., :512].reshape(b, s, 4, 128))
    ak = _l2(y[..., 512:1024].reshape(b, s, 4, 128))
    av = y[..., 1024:].reshape(b, s, 4, 128)
    beta = jax.nn.sigmoid(small[..., :8]).reshape(b, s, 2, 4)
    ga = -jnp.exp(a_log) * jax.nn.softplus(small[..., 8:16].reshape(b, s, 2, 4) + dt_bias)
    o = (_gdn_jax(aq, ak, av, beta[:, :, 0], ga[:, :, 0])
         + _flip(_gdn_jax(_flip(aq), _flip(ak), _flip(av), _flip(beta[:, :, 1]), _flip(ga[:, :, 1]))))
    return (_rms(o, gdn_norm_g).reshape(b, s, 512) * jax.nn.silu(col(CB_AZ, 4))).astype(BF16)


def _mixer_b_jax(proj, lb, hgrn_norm_g):
    b, s, _ = proj.shape
    col = lambda cb, n: proj[:, :, cb * LANES:(cb + n) * LANES].astype(F32)
    bq = jax.nn.silu(col(CB_BQ, 4)).reshape(b, s, 4, 128)
    bi = col(CB_BI, 4).reshape(b, s, 4, 128)
    bf = col(CB_BF, 8).reshape(b, s, 2, 4, 128)
    kk = (1.0 - lb.reshape(2, 4, 128)) * jax.nn.sigmoid(-bf)
    log_f = jnp.log1p(-jnp.minimum(kk, 1.0 - F_MIN_GAP))
    o = (_hgrn_jax(bq, kk[:, :, 0], bi, log_f[:, :, 0])
         + _flip(_hgrn_jax(_flip(bq), _flip(kk[:, :, 1]), _flip(bi), _flip(log_f[:, :, 1]))))
    return (_rms(o, hgrn_norm_g).reshape(b, s, 512) * jax.nn.silu(col(CB_BZ, 4))).astype(BF16)


def _rope_tables(seq):
    half = ROPE_DIM // 2
    inv = 1.0 / (ROPE_THETA ** (jnp.arange(0, ROPE_DIM, 2, dtype=F32) / ROPE_DIM))
    ang = jnp.arange(seq, dtype=F32)[:, None] * inv[None, :]
    cos, sin = jnp.cos(ang), jnp.sin(ang)
    one = jnp.ones((seq, D_C - ROPE_DIM), F32)
    zero = jnp.zeros((seq, D_C - ROPE_DIM), F32)
    zh = jnp.zeros((seq, half), F32)
    c_map = jnp.concatenate([cos, cos, one], axis=1)
    s1_map = jnp.concatenate([zh, sin, zero], axis=1)
    s2_map = jnp.concatenate([-sin, zh, zero], axis=1)
    tile2 = lambda t: jnp.concatenate([t, t], axis=1)
    return tile2(c_map), tile2(s1_map), tile2(s2_map)


def _lower_bounds(lb_logits):
    p = jax.nn.softmax(lb_logits.astype(F32), axis=1)
    return jnp.cumsum(p, axis=1) - p[:, :1]


def kernel(x_prompt, x_sample, norm_g, w_in, conv_w, a_log, dt_bias, gdn_norm_g, hgrn_lb_logits, hgrn_norm_g,
           q_norm_g, k_norm_g, diff_lambda, subln_g, w_br_a, w_br_b, w_br_c, w_out):
    nb_p = x_prompt.shape[0]
    x = jnp.concatenate([x_prompt, x_sample], axis=0)
    b, s, d = x.shape
    depth = w_in.shape[0]
    n_in = w_in.shape[-1]
    gate0 = n_in - 3 * D_MODEL
    w_main = jnp.concatenate([w_in[:, :, gate0:], w_in[:, :, :2048], w_in[:, :, 2048 + N_SMALL:gate0]],
                             axis=-1).astype(BF16)
    w_small = jnp.pad(w_in[:, :, 2048:2048 + N_SMALL], ((0, 0), (0, 0), (0, LANES - N_SMALL))).astype(BF16)
    wa, wb, wc, wo = (w.astype(BF16) for w in (w_br_a, w_br_b, w_br_c, w_out))
    cos_t, s1_t, s2_t = _rope_tables(s)
    lbs = _lower_bounds(hgrn_lb_logits)
    lane_pad = ((0, 0), (2 * N_HEADS, LANES - 4 * N_HEADS))
    neg_a_vec = jnp.pad(-jnp.exp(a_log.astype(F32)).reshape(depth, 2 * N_HEADS), lane_pad)[:, None, :]
    dtb_vec = jnp.pad(dt_bias.astype(F32).reshape(depth, 2 * N_HEADS), lane_pad)[:, None, :]
    lp = diff_lambda.astype(F32)
    lam_dyn = jnp.exp(jnp.sum(lp[:, 0] * lp[:, 1], axis=-1)) - jnp.exp(jnp.sum(lp[:, 2] * lp[:, 3], axis=-1))

    x2d = x.reshape(b * s, d)
    for l in range(depth):
        lambda_init = 0.8 - 0.6 * math.exp(-0.3 * l)
        proj2d, small2d = _inproj(x2d, norm_g[l][None, :], w_main[l], w_small[l])
        proj = proj2d.reshape(b, s, N_MAIN)
        small = small2d.reshape(b, s, LANES)
        small_t = small[:, :, :N_SMALL].transpose(0, 2, 1).reshape(b, N_SMALL, s // CHUNK_A, CHUNK_A)
        ya = _mixer_a(proj, small, small_t, conv_w[l], a_log[l], dt_bias[l], neg_a_vec[l], dtb_vec[l],
                      gdn_norm_g[l][None, :])
        yb = _mixer_b(proj, lbs[:, l], hgrn_norm_g[l][None, :])
        qn, kn = _qkprep(proj, cos_t, s1_t, s2_t, q_norm_g[l].reshape(1, LANES), k_norm_g[l].reshape(1, LANES))
        scal = jnp.stack([lam_dyn[l] + lambda_init, jnp.asarray(1.0 - lambda_init, F32)]).astype(F32)
        yc = _attn(scal, qn, kn, proj, subln_g[l][None, :])
        x2d = _merge(x2d, ya.reshape(b * s, -1), yb.reshape(b * s, -1), yc.reshape(b * s, -1), proj2d,
                     wa[l], wb[l], wc[l], wo[l])
    y = x2d.reshape(b, s, d)
    return (y[:nb_p], y[nb_p:])
```

```python
import functools
import math

import jax
import jax.numpy as jnp
from jax import lax
from jax.experimental import pallas as pl
from jax.experimental.pallas import tpu as pltpu

F32 = jnp.float32
BF16 = jnp.bfloat16
EPS = 1e-6
F_MIN_GAP = 1e-6
LOG2E = math.log2(math.e)

D_MODEL = 1024
LANES = 128
SUB = 8
N_HEADS = 4
HEAD_W = 128
CONV_K = 5
CHUNK_A = 64
CHUNK_B = 16
D_C = 64
ROPE_DIM = D_C // 4
ROPE_THETA = 500000.0
N_SMALL = 16
N_MAIN = 9728
CB_AQ, CB_AK, CB_AV, CB_AZ = 24, 28, 32, 36
CB_BQ, CB_BI, CB_BF, CB_BZ = 40, 44, 48, 56
CB_CQ, CB_CK, CB_CV, CB_CZ = 60, 64, 68, 72
VMEM_LIMIT = 56 * 1024 * 1024


def _cparams(sem):
    return pltpu.CompilerParams(dimension_semantics=sem, vmem_limit_bytes=VMEM_LIMIT)


def _inproj_body(x_ref, g_ref, w_ref, ws_ref, o_ref, os_ref, h_ref):
    @pl.when(pl.program_id(1) == 0)
    def _():
        x = x_ref[...]
        h = x * lax.rsqrt(jnp.mean(x * x, axis=-1, keepdims=True) + EPS) * g_ref[...]
        hb = h.astype(BF16)
        h_ref[...] = hb
        os_ref[...] = jnp.dot(hb, ws_ref[...], preferred_element_type=F32)

    o_ref[...] = jnp.dot(h_ref[...], w_ref[...], preferred_element_type=F32).astype(BF16)


def _inproj(x2d, g, w_main, w_small, tm=1024, tn=2432):
    t = x2d.shape[0]
    while t % tm:
        tm //= 2
    return pl.pallas_call(
        _inproj_body,
        grid=(t // tm, N_MAIN // tn),
        in_specs=[
            pl.BlockSpec((tm, D_MODEL), lambda i, j: (i, 0)),
            pl.BlockSpec((1, D_MODEL), lambda i, j: (0, 0)),
            pl.BlockSpec((D_MODEL, tn), lambda i, j: (0, j)),
            pl.BlockSpec((D_MODEL, LANES), lambda i, j: (0, 0)),
        ],
        out_specs=[
            pl.BlockSpec((tm, tn), lambda i, j: (i, j)),
            pl.BlockSpec((tm, LANES), lambda i, j: (i, 0)),
        ],
        out_shape=[
            jax.ShapeDtypeStruct((t, N_MAIN), BF16),
            jax.ShapeDtypeStruct((t, LANES), F32),
        ],
        scratch_shapes=[pltpu.VMEM((tm, D_MODEL), BF16)],
        compiler_params=_cparams(("parallel", "arbitrary")),
        name="inproj",
    )(x2d, g, w_main, w_small)


def _qkprep_body(q_ref, k_ref, c_ref, s1_ref, s2_ref, qg_ref, kg_ref, qo_ref, ko_ref, *, q_scale):
    lane = lax.broadcasted_iota(jnp.int32, (1, LANES), 1)
    cos, s1, s2 = c_ref[...], s1_ref[...], s2_ref[...]

    def prep(x, g):
        x = x.astype(F32)
        x2 = x * x
        lo = jnp.sum(x2[:, :D_C], axis=-1, keepdims=True)
        hi = jnp.sum(x2[:, D_C:], axis=-1, keepdims=True)
        ms = jnp.where(lane < D_C, lo, hi) * (1.0 / D_C)
        y = x * lax.rsqrt(ms + EPS) * g
        return y * cos + pltpu.roll(y, ROPE_DIM // 2, 1) * s1 + pltpu.roll(y, LANES - ROPE_DIM // 2, 1) * s2

    qo_ref[0] = (prep(q_ref[0], qg_ref[...]) * q_scale).astype(BF16)
    ko_ref[0] = prep(k_ref[0], kg_ref[...]).astype(BF16)


def _qkprep(proj, cos_t, s1_t, s2_t, qg, kg, ts=1024):
    b, s, _ = proj.shape
    ts = min(ts, s)
    q_scale = (D_C ** -0.5) * math.log2(math.e)
    tab = pl.BlockSpec((ts, LANES), lambda bi, si, h: (si, 0))
    vec = pl.BlockSpec((1, LANES), lambda bi, si, h: (0, 0))
    out = pl.BlockSpec((1, ts, LANES), lambda bi, si, h: (bi, si, h))
    return pl.pallas_call(
        functools.partial(_qkprep_body, q_scale=q_scale),
        grid=(b, s // ts, N_HEADS),
        in_specs=[
            pl.BlockSpec((1, ts, LANES), lambda bi, si, h: (bi, si, CB_CQ + h)),
            pl.BlockSpec((1, ts, LANES), lambda bi, si, h: (bi, si, CB_CK + h)),
            tab, tab, tab, vec, vec,
        ],
        out_specs=[out, out],
        out_shape=[jax.ShapeDtypeStruct((b, s, N_HEADS * HEAD_W), BF16)] * 2,
        compiler_params=_cparams(("parallel", "parallel", "parallel")),
        name="qkprep",
    )(proj, proj, cos_t, s1_t, s2_t, qg, kg)


def _attn_body(sc_ref, q_ref, k_ref, v_ref, z_ref, g_ref, o_ref):
    lam = sc_ref[0]
    out_scale = sc_ref[1]
    q = q_ref[0]
    k = k_ref[0]
    v = v_ref[0]
    lane = lax.broadcasted_iota(jnp.int32, (1, LANES), 1)
    zero = jnp.zeros_like(q)

    def one_map(qm):
        s = lax.dot_general(qm, k, (((1,), (1,)), ((), ())), preferred_element_type=F32)
        m = jnp.max(s, axis=-1, keepdims=True)
        p = jnp.exp2(s - m)
        l = jnp.sum(p, axis=-1, keepdims=True)
        return jnp.dot(p.astype(BF16), v, preferred_element_type=F32) / l

    o = one_map(jnp.where(lane < D_C, q, zero)) - lam * one_map(jnp.where(lane >= D_C, q, zero))
    y = o * lax.rsqrt(jnp.mean(o * o, axis=-1, keepdims=True) + EPS) * g_ref[...]
    z = z_ref[0].astype(F32)
    o_ref[0] = (y * out_scale * (z * jax.nn.sigmoid(z))).astype(BF16)


def _attn(scal, qn, kn, proj, subln_g, tq=256):
    b, s, _ = proj.shape
    tq = min(tq, s)
    return pl.pallas_call(
        _attn_body,
        grid=(b, N_HEADS, s // tq),
        in_specs=[
            pl.BlockSpec(memory_space=pltpu.SMEM),
            pl.BlockSpec((1, tq, LANES), lambda bi, h, qi: (bi, qi, h)),
            pl.BlockSpec((1, s, LANES), lambda bi, h, qi: (bi, 0, h)),
            pl.BlockSpec((1, s, LANES), lambda bi, h, qi: (bi, 0, CB_CV + h)),
            pl.BlockSpec((1, tq, LANES), lambda bi, h, qi: (bi, qi, CB_CZ + h)),
            pl.BlockSpec((1, LANES), lambda bi, h, qi: (0, 0)),
        ],
        out_specs=pl.BlockSpec((1, tq, LANES), lambda bi, h, qi: (bi, qi, h)),
        out_shape=jax.ShapeDtypeStruct((b, s, N_HEADS * HEAD_W), BF16),
        compiler_params=_cparams(("parallel", "parallel", "arbitrary")),
        name="diffattn",
    )(scal, qn, kn, proj, proj, subln_g)


def _merge_body(x_ref, ya_ref, yb_ref, yc_ref, g0_ref, g1_ref, g2_ref, wa_ref, wb_ref, wc_ref, wo_ref, o_ref):
    def branch(y_ref, w_ref, g_ref):
        return jax.nn.sigmoid(g_ref[...].astype(F32)) * jnp.dot(y_ref[...], w_ref[...], preferred_element_type=F32)

    merged = branch(ya_ref, wa_ref, g0_ref) + branch(yb_ref, wb_ref, g1_ref) + branch(yc_ref, wc_ref, g2_ref)
    o_ref[...] = x_ref[...] + jnp.dot(merged.astype(BF16), wo_ref[...], preferred_element_type=F32)


def _merge(x2d, ya, yb, yc, proj2d, wa, wb, wc, wo, tm=512):
    t = x2d.shape[0]
    w_br = N_HEADS * HEAD_W
    row = lambda i: (i, 0)
    full = lambda i: (0, 0)
    y_spec = pl.BlockSpec((tm, w_br), row)
    wbr_spec = pl.BlockSpec((w_br, D_MODEL), full)
    return pl.pallas_call(
        _merge_body,
        grid=(t // tm,),
        in_specs=[
            pl.BlockSpec((tm, D_MODEL), row),
            y_spec, y_spec, y_spec,
            pl.BlockSpec((tm, D_MODEL), lambda i: (i, 0)),
            pl.BlockSpec((tm, D_MODEL), lambda i: (i, 1)),
            pl.BlockSpec((tm, D_MODEL), lambda i: (i, 2)),
            wbr_spec, wbr_spec, wbr_spec,
            pl.BlockSpec((D_MODEL, D_MODEL), full),
        ],
        out_specs=pl.BlockSpec((tm, D_MODEL), row),
        out_shape=jax.ShapeDtypeStruct((t, D_MODEL), F32),
        compiler_params=_cparams(("parallel",)),
        name="merge",
    )(x2d, ya, yb, yc, proj2d, proj2d, proj2d, wa, wb, wc, wo)


def _split3(x):
    p1 = x.astype(BF16)
    r1 = x - p1.astype(F32)
    p2 = r1.astype(BF16)
    p3 = (r1 - p2.astype(F32)).astype(BF16)
    return p1, p2, p3


def _softplus(x):
    return jnp.maximum(x, 0.0) + jnp.log1p(jnp.exp(-jnp.abs(x)))


def _silu(x):
    return x * jax.nn.sigmoid(x)


def _dot(a, b):
    return jnp.dot(a, b, preferred_element_type=F32)


def _dot_nt(a, b):
    return lax.dot_general(a, b, (((1,), (1,)), ((), ())), preferred_element_type=F32)


CUM_TILE = 256
GDN_GROUP = 4


def _gdn_body(alog_ref, dtb_ref, q_ref, k_ref, v_ref, z_ref, cwq_ref, cwk_ref, cwv_ref, sm_ref, smt_ref,
              nav_ref, dtv_ref, ng_ref, o_ref,
              xp_ref, qn_ref, kn_ref, vn_ref, gcc_ref, bc_ref, gcr_ref, br_ref,
              lhs_ref, r_ref, gl_ref, of_ref, ob_ref):
    h = pl.program_id(1)
    seq = q_ref.shape[1]
    c = CHUNK_A
    n_chunks = seq // c
    dk = HEAD_W

    row_t = lax.broadcasted_iota(jnp.int32, (CUM_TILE, CUM_TILE), 0)
    col_t = lax.broadcasted_iota(jnp.int32, (CUM_TILE, CUM_TILE), 1)
    same_chunk = (row_t // c) == (col_t // c)
    tri = (jnp.where(same_chunk & (col_t <= row_t), 1.0, 0.0).astype(BF16),
           jnp.where(same_chunk & (col_t >= row_t), 1.0, 0.0).astype(BF16))
    sel_r = lax.broadcasted_iota(jnp.int32, (LANES, LANES), 0)

    def gates_tile(t, carry):
        rows = pl.ds(pl.multiple_of(t * CUM_TILE, CUM_TILE), CUM_TILE)
        sm = sm_ref[0, rows, :]
        g_parts = _split3(nav_ref[...] * _softplus(sm + dtv_ref[...]))
        b_parts = _split3(jax.nn.sigmoid(sm))
        for d in range(2):
            sel_g = jnp.where(sel_r == 8 + 4 * d + h, 1.0, 0.0).astype(BF16)
            sel_b = jnp.where(sel_r == 4 * d + h, 1.0, 0.0).astype(BF16)
            gc = jnp.zeros((CUM_TILE, LANES), F32)
            bt = jnp.zeros((CUM_TILE, LANES), F32)
            for p in range(3):
                gc = gc + _dot(tri[d], _dot(g_parts[p], sel_g).astype(BF16))
                bt = bt + _dot(b_parts[p], sel_b)
            gcc_ref[d, rows, :] = gc
            bc_ref[d, rows, :] = bt
        return carry

    lax.fori_loop(0, seq // CUM_TILE, gates_tile, 0)

    rc = lax.broadcasted_iota(jnp.int32, (c, c), 0)
    cc = lax.broadcasted_iota(jnp.int32, (c, c), 1)
    for d in range(2):
        a_neg = -jnp.exp(alog_ref[d, h])
        g = a_neg * _softplus(smt_ref[0, 8 + 4 * d + h] + dtb_ref[d, h])
        cum = jnp.where((rc <= cc) if d == 0 else (rc >= cc), 1.0, 0.0).astype(BF16)
        acc = jnp.zeros((n_chunks, c), F32)
        for part in _split3(g):
            acc = acc + _dot(part, cum)
        gcr_ref[d] = acc
        br_ref[d] = jax.nn.sigmoid(smt_ref[0, 4 * d + h])

    pad = 8
    zeros_pad = jnp.zeros((pad, LANES), F32)

    def conv_into(src_ref, cw_ref, dst_ref, normalise, scale):
        xp_ref[pl.ds(0, pad), :] = zeros_pad
        xp_ref[pl.ds(pad + seq, pad), :] = zeros_pad
        xp_ref[pl.ds(pad, seq), :] = src_ref[0].astype(F32)
        cw = cw_ref[...]

        def chunk(n, carry):
            base = pl.multiple_of(n * c, c)
            win = xp_ref[pl.ds(base, c + 2 * pad), :]
            y = jnp.zeros((c, LANES), F32)
            for j in range(CONV_K):
                off = pad - (CONV_K - 1) // 2 + j
                y = y + win[off:off + c, :] * cw[j:j + 1, :]
            y = _silu(y)
            if normalise:
                y = y * lax.rsqrt(jnp.sum(y * y, axis=-1, keepdims=True) + EPS)
            dst_ref[pl.ds(base, c), :] = y * scale if scale != 1.0 else y
            return carry

        lax.fori_loop(0, n_chunks, chunk, 0, unroll=4)

    conv_into(q_ref, cwq_ref, qn_ref, True, dk ** -0.5)
    conv_into(k_ref, cwk_ref, kn_ref, True, 1.0)
    conv_into(v_ref, cwv_ref, vn_ref, False, 1.0)

    eye = jnp.where(rc == cc, 1.0, 0.0)
    masks = (((rc >= cc), (rc > cc)), ((rc <= cc), (rc < cc)))

    def prep_group(i, carry):
        chains = []
        for g in range(GDN_GROUP):
            n = i * GDN_GROUP + g
            rows = pl.ds(pl.multiple_of(n * c, c), c)
            q = qn_ref[rows, :]
            k = kn_ref[rows, :]
            k16 = k.astype(BF16)
            v16 = vn_ref[rows, :].astype(BF16)
            kk = _dot_nt(k16, k16)
            qk = _dot_nt(q.astype(BF16), k16)
            for d in range(2):
                incl, strict = masks[d]
                gcol = gcc_ref[d, rows, :]
                grow = gcr_ref[d, pl.ds(n, 1), :]
                decay = jnp.where(incl, jnp.exp(jnp.where(incl, gcol[:, :c] - grow, 0.0)), 0.0)
                x = -jnp.where(strict, kk * decay * bc_ref[d, rows, :][:, :c], 0.0)
                last = (c - 1) if d == 0 else 0
                g_last = gcol[last:last + 1, :]
                gl_ref[d, pl.ds(n, 1), :] = jnp.exp(g_last)
                chains.append(dict(d=d, n=n, rows=rows, x=x, grow=grow, k16=k16, v16=v16,
                                   a16=(qk * decay).astype(BF16), qd=q * jnp.exp(gcol),
                                   kdt16=(k * jnp.exp(g_last - gcol)).T.astype(BF16)))
        t_inv = [eye + ch["x"] for ch in chains]
        pw16 = [ch["x"].astype(BF16) for ch in chains]
        pw16 = [_dot(p, p).astype(BF16) for p in pw16]
        for _ in range(4):
            t_inv = [t + _dot(t.astype(BF16), p) for t, p in zip(t_inv, pw16)]
            pw16 = [_dot(p, p).astype(BF16) for p in pw16]
        t_inv = [t + _dot(t.astype(BF16), p) for t, p in zip(t_inv, pw16)]
        uw = []
        for ch, t in zip(chains, t_inv):
            tb = t * br_ref[ch["d"], pl.ds(ch["n"], 1), :]
            uw.append((_dot(tb.astype(BF16), ch["v16"]).astype(BF16),
                       _dot((tb * jnp.exp(ch["grow"])).astype(BF16), ch["k16"]).astype(BF16)))
        for ch, (u16, w16) in zip(chains, uw):
            d, n = ch["d"], ch["n"]
            lhs_ref[d, n, pl.ds(0, dk), :] = _dot(ch["kdt16"], w16).astype(BF16)
            lhs_ref[d, n, pl.ds(dk, c), :] = (ch["qd"] - _dot(ch["a16"], w16)).astype(BF16)
            r_ref[d, n] = _dot(ch["kdt16"], u16).astype(BF16)
            (of_ref if d == 0 else ob_ref)[ch["rows"], :] = _dot(ch["a16"], u16)
        return carry

    lax.fori_loop(0, n_chunks // GDN_GROUP, prep_group, 0)

    def scan_step(i, carry):
        outs = []
        for d, s, o_out in ((0, carry[0], of_ref), (1, carry[1], ob_ref)):
            n = i if d == 0 else n_chunks - 1 - i
            rows = pl.ds(pl.multiple_of(n * c, c), c)
            res = _dot(lhs_ref[d, n], s.astype(BF16))
            o_out[rows, :] = o_out[rows, :] + res[dk:, :]
            outs.append(s * gl_ref[d, pl.ds(n, 1), :] + r_ref[d, n].astype(F32) - res[:dk, :])
        return tuple(outs)

    s0 = jnp.zeros((dk, HEAD_W), F32)
    lax.fori_loop(0, n_chunks, scan_step, (s0, s0))

    def finish(t, carry):
        rows = pl.ds(pl.multiple_of(t * CUM_TILE, CUM_TILE), CUM_TILE)
        o = of_ref[rows, :] + ob_ref[rows, :]
        y = o * lax.rsqrt(jnp.mean(o * o, axis=-1, keepdims=True) + EPS) * ng_ref[...]
        o_ref[0, rows, :] = (y * _silu(z_ref[0, rows, :].astype(F32))).astype(BF16)
        return carry

    lax.fori_loop(0, seq // CUM_TILE, finish, 0)


def _mixer_a(proj, small, small_t, conv_w, a_log, dt_bias, neg_a_vec, dtb_vec, norm_g):
    b, s, _ = proj.shape
    n_chunks = s // CHUNK_A
    col = lambda cb: pl.BlockSpec((1, s, LANES), lambda bi, h: (bi, 0, cb + h))
    cw = lambda cb: pl.BlockSpec((CONV_K, LANES), lambda bi, h: (0, cb + h))
    vec = pl.BlockSpec((1, LANES), lambda bi, h: (0, 0))
    smem = pl.BlockSpec(memory_space=pltpu.SMEM)
    return pl.pallas_call(
        _gdn_body,
        grid=(b, N_HEADS),
        in_specs=[
            smem, smem,
            col(CB_AQ), col(CB_AK), col(CB_AV), col(CB_AZ),
            cw(0), cw(4), cw(8),
            pl.BlockSpec((1, s, LANES), lambda bi, h: (bi, 0, 0)),
            pl.BlockSpec((1, N_SMALL, n_chunks, CHUNK_A), lambda bi, h: (bi, 0, 0, 0)),
            vec, vec, vec,
        ],
        out_specs=pl.BlockSpec((1, s, LANES), lambda bi, h: (bi, 0, h)),
        out_shape=jax.ShapeDtypeStruct((b, s, N_HEADS * HEAD_W), BF16),
        scratch_shapes=[
            pltpu.VMEM((s + 16, LANES), F32),
            pltpu.VMEM((s, LANES), F32),
            pltpu.VMEM((s, LANES), F32),
            pltpu.VMEM((s, LANES), F32),
            pltpu.VMEM((2, s, LANES), F32),
            pltpu.VMEM((2, s, LANES), F32),
            pltpu.VMEM((2, n_chunks, CHUNK_A), F32),
            pltpu.VMEM((2, n_chunks, CHUNK_A), F32),
            pltpu.VMEM((2, n_chunks, HEAD_W + CHUNK_A, LANES), BF16),
            pltpu.VMEM((2, n_chunks, HEAD_W, LANES), BF16),
            pltpu.VMEM((2, n_chunks, LANES), F32),
            pltpu.VMEM((s, LANES), F32),
            pltpu.VMEM((s, LANES), F32),
        ],
        compiler_params=_cparams(("parallel", "arbitrary")),
        name="gdn",
    )(a_log, dt_bias, proj, proj, proj, proj, conv_w, conv_w, conv_w, small, small_t, neg_a_vec, dtb_vec, norm_g)


HGRN_GROUP = 4


def _hgrn_body(q_ref, i_ref, ff_ref, fb_ref, z_ref, lb_ref, ng_ref, o_ref,
               qs_ref, kk_ref, gc_ref, gt_ref, qd_ref, kd_ref, of_ref, ob_ref):
    seq = q_ref.shape[1]
    c = CHUNK_B
    n_chunks = seq // c
    dk = HEAD_W

    row_t = lax.broadcasted_iota(jnp.int32, (CUM_TILE, CUM_TILE), 0)
    col_t = lax.broadcasted_iota(jnp.int32, (CUM_TILE, CUM_TILE), 1)
    same_chunk = (row_t // c) == (col_t // c)
    tri = (jnp.where(same_chunk & (col_t <= row_t), 1.0, 0.0).astype(BF16),
           jnp.where(same_chunk & (col_t >= row_t), 1.0, 0.0).astype(BF16))
    tot = jnp.where(same_chunk, 1.0, 0.0).astype(BF16)

    def gates_tile(t, carry):
        rows = pl.ds(pl.multiple_of(t * CUM_TILE, CUM_TILE), CUM_TILE)
        qs = _silu(q_ref[0, rows, :].astype(F32)) * (dk ** -0.5)
        qs_ref[rows, :] = qs
        for d, f_ref in ((0, ff_ref), (1, fb_ref)):
            bf = f_ref[0, rows, :].astype(F32)
            kk = (1.0 - lb_ref[d:d + 1, :]) * jax.nn.sigmoid(-bf)
            log_f = jnp.log1p(-jnp.minimum(kk, 1.0 - F_MIN_GAP))
            gc = jnp.zeros((CUM_TILE, LANES), F32)
            gt = jnp.zeros((CUM_TILE, LANES), F32)
            for part in _split3(log_f):
                gc = gc + _dot(tri[d], part)
                gt = gt + _dot(tot, part)
            kk_ref[d, rows, :] = kk
            gc_ref[d, rows, :] = gc * LOG2E
            gt_ref[d, rows, :] = gt
            qd_ref[d, rows, :] = (qs * jnp.exp(gc)).astype(BF16)
            kd_ref[d, rows, :] = (kk * jnp.exp(gt - gc)).astype(BF16)
        return carry

    lax.fori_loop(0, seq // CUM_TILE, gates_tile, 0)

    jrow = lax.broadcasted_iota(jnp.int32, (SUB, LANES), 0)
    ones_w = jnp.ones((LANES, LANES), BF16)
    n_pieces = c + c // 2
    piece = lax.broadcasted_iota(jnp.int32, (c, n_pieces * SUB), 1) // SUB
    out_row = lax.broadcasted_iota(jnp.int32, (c, n_pieces * SUB), 0)
    row_of_piece = (jnp.where(piece < SUB, piece, SUB + (piece - SUB) // 2),
                    jnp.where(piece < c, piece // 2, piece - SUB))
    sel = tuple(jnp.where(rp == out_row, 1.0, 0.0).astype(BF16) for rp in row_of_piece)

    def scan_step(i, carry):
        work = []
        for g in range(HGRN_GROUP):
            for d in range(2):
                n = i * HGRN_GROUP + g
                n = n if d == 0 else n_chunks - 1 - n
                work.append((d, pl.ds(pl.multiple_of(n * c, c), c)))
        tiles, vs, upds, es = [], [], [], []
        for d, rows in work:
            qs = qs_ref[rows, :]
            kk = kk_ref[d, rows, :]
            gc2 = gc_ref[d, rows, :]
            v16 = i_ref[0, rows, :]
            v = v16.astype(F32)
            parts, vparts = [], []
            for r in range(c):
                for j0 in (0, SUB):
                    if (j0 > r) if d == 0 else (j0 + SUB - 1 < r):
                        continue
                    pair = jnp.exp2(gc2[r:r + 1, :] - gc2[j0:j0 + SUB, :])
                    if not ((j0 + SUB - 1 <= r) if d == 0 else (j0 >= r)):
                        mask = (jrow + j0 <= r) if d == 0 else (jrow + j0 >= r)
                        pair = jnp.where(mask, pair, 0.0)
                    parts.append(pair * kk[j0:j0 + SUB, :] * qs[r:r + 1, :])
                    vparts.append(v[j0:j0 + SUB, :])
            tiles.append(jnp.concatenate(parts, axis=0).astype(BF16))
            vs.append(jnp.concatenate(vparts, axis=0))
            upds.append(lax.dot_general(v16, kd_ref[d, rows, :], (((0,), (0,)), ((), ())),
                                        preferred_element_type=F32))
            es.append(jnp.exp(gt_ref[d, pl.ds(rows.start, 1), :]))
        a_reps = [_dot(t, ones_w) for t in tiles]
        ps = [(a * v).astype(BF16) for a, v in zip(a_reps, vs)]
        o_intra = [_dot(sel[d], p) for (d, _), p in zip(work, ps)]
        st = list(carry)
        st16 = []
        for (d, rows), upd, e_row in zip(work, upds, es):
            st16.append(st[d].astype(BF16))
            st[d] = st[d] * e_row + upd
        for (d, rows), oi, s16 in zip(work, o_intra, st16):
            o_out = of_ref if d == 0 else ob_ref
            o_out[rows, :] = oi + _dot_nt(qd_ref[d, rows, :], s16)
        return tuple(st)

    s0 = jnp.zeros((HEAD_W, dk), F32)
    lax.fori_loop(0, n_chunks // HGRN_GROUP, scan_step, (s0, s0))

    def finish(t, carry):
        rows = pl.ds(pl.multiple_of(t * CUM_TILE, CUM_TILE), CUM_TILE)
        o = of_ref[rows, :] + ob_ref[rows, :]
        y = o * lax.rsqrt(jnp.mean(o * o, axis=-1, keepdims=True) + EPS) * ng_ref[...]
        o_ref[0, rows, :] = (y * _silu(z_ref[0, rows, :].astype(F32))).astype(BF16)
        return carry

    lax.fori_loop(0, seq // CUM_TILE, finish, 0)


def _mixer_b(proj, lb, norm_g):
    b, s, _ = proj.shape
    col = lambda cb: pl.BlockSpec((1, s, LANES), lambda bi, h: (bi, 0, cb + h))
    return pl.pallas_call(
        _hgrn_body,
        grid=(b, N_HEADS),
        in_specs=[
            col(CB_BQ), col(CB_BI), col(CB_BF), col(CB_BF + N_HEADS), col(CB_BZ),
            pl.BlockSpec((2, LANES), lambda bi, h: (0, h)),
            pl.BlockSpec((1, LANES), lambda bi, h: (0, 0)),
        ],
        out_specs=pl.BlockSpec((1, s, LANES), lambda bi, h: (bi, 0, h)),
        out_shape=jax.ShapeDtypeStruct((b, s, N_HEADS * HEAD_W), BF16),
        scratch_shapes=[
            pltpu.VMEM((s, LANES), F32),
            pltpu.VMEM((2, s, LANES), F32),
            pltpu.VMEM((2, s, LANES), F32),
            pltpu.VMEM((2, s, LANES), F32),
            pltpu.VMEM((2, s, LANES), BF16),
            pltpu.VMEM((2, s, LANES), BF16),
            pltpu.VMEM((s, LANES), F32),
            pltpu.VMEM((s, LANES), F32),
        ],
        compiler_params=_cparams(("parallel", "arbitrary")),
        name="hgrn",
    )(proj, proj, proj, proj, proj, lb, norm_g)


def _rms(x, g):
    return x * lax.rsqrt(jnp.mean(x * x, axis=-1, keepdims=True) + EPS) * g


def _l2(x):
    return x * lax.rsqrt(jnp.sum(x * x, axis=-1, keepdims=True) + EPS)


def _to_chunks(t, c):
    b, s, h, d = t.shape
    return t.reshape(b, s // c, c, h, d).transpose(1, 0, 3, 2, 4)


def _from_chunks(t):
    n, b, h, c, d = t.shape
    return t.transpose(1, 0, 3, 2, 4).reshape(b, n * c, h, d)


def _mexp(diff, mask):
    return jnp.where(mask, jnp.exp(jnp.where(mask, diff, 0.0)), 0.0)


def _gdn_jax(q, k, v, beta, g):
    bsz, _, nh, dk = k.shape
    dv = v.shape[-1]
    q, k, v = (_to_chunks(t, CHUNK_A) for t in (q, k, v))
    beta = _to_chunks(beta[..., None], CHUNK_A)[..., 0]
    gc = jnp.cumsum(_to_chunks(g[..., None], CHUNK_A)[..., 0], axis=-1)
    incl = jnp.tril(jnp.ones((CHUNK_A, CHUNK_A), bool))
    strict = jnp.tril(jnp.ones((CHUNK_A, CHUNK_A), bool), -1)
    decay = _mexp(gc[..., :, None] - gc[..., None, :], incl)
    kb = k * beta[..., None]
    m = jnp.where(strict, jnp.einsum('nbhid,nbhjd->nbhij', kb, k) * decay, 0.0)
    lhs = m + jnp.eye(CHUNK_A, dtype=F32)
    rhs = jnp.concatenate([v * beta[..., None], kb * jnp.exp(gc)[..., None]], axis=-1)
    sol = lax.linalg.triangular_solve(lhs, rhs, left_side=True, lower=True)
    u, w = sol[..., :dv], sol[..., dv:]
    qs = q * (dk ** -0.5)
    a_qk = jnp.einsum('nbhid,nbhjd->nbhij', qs, k) * decay
    q_dec = qs * jnp.exp(gc)[..., None]
    g_last = gc[..., -1]
    k_dec = k * jnp.exp(g_last[..., None] - gc)[..., None]

    def step(s, inp):
        u_c, w_c, qd_c, a_c, kd_c, gl_c = inp
        v_new = u_c - jnp.einsum('bhck,bhkv->bhcv', w_c, s)
        o = jnp.einsum('bhck,bhkv->bhcv', qd_c, s) + jnp.einsum('bhij,bhjv->bhiv', a_c, v_new)
        s = s * jnp.exp(gl_c)[..., None, None] + jnp.einsum('bhck,bhcv->bhkv', kd_c, v_new)
        return s, o

    s0 = jnp.zeros((bsz, nh, dk, dv), F32)
    _, o = lax.scan(step, s0, (u, w, q_dec, a_qk, k_dec, g_last))
    return _from_chunks(o)


def _hgrn_jax(q, k, v, log_f):
    bsz, _, nh, dk = q.shape
    dv = v.shape[-1]
    q, k, v, log_f = (_to_chunks(t, CHUNK_B) for t in (q, k, v, log_f))
    gc = jnp.cumsum(log_f, axis=-2)
    qs = q * (dk ** -0.5)
    q_dec = qs * jnp.exp(gc)
    g_last = gc[..., -1, :]
    k_dec = k * jnp.exp(g_last[..., None, :] - gc)
    incl = jnp.tril(jnp.ones((CHUNK_B, CHUNK_B), bool))[:, :, None]

    def step(s, inp):
        qs_c, qd_c, k_c, kd_c, v_c, gc_c, gl_c = inp
        pair = _mexp(gc_c[:, :, :, None, :] - gc_c[:, :, None, :, :], incl)
        a = jnp.einsum('bhik,bhjk,bhijk->bhij', qs_c, k_c, pair)
        o = jnp.einsum('bhck,bhkv->bhcv', qd_c, s) + jnp.einsum('bhij,bhjv->bhiv', a, v_c)
        s = s * jnp.exp(gl_c)[..., None] + jnp.einsum('bhck,bhcv->bhkv', kd_c, v_c)
        return s, o

    s0 = jnp.zeros((bsz, nh, dk, dv), F32)
    _, o = lax.scan(step, s0, (qs, q_dec, k, k_dec, v, gc, g_last))
    return _from_chunks(o)


def _flip(t):
    return jnp.flip(t, axis=1)


def _mixer_a_jax(proj, small, conv_w, a_log, dt_bias, gdn_norm_g):
    b, s, _ = proj.shape
    col = lambda cb, n: proj[:, :, cb * LANES:(cb + n) * LANES].astype(F32)
    qkv = col(CB_AQ, 12)
    y = lax.conv_general_dilated(qkv, conv_w[:, None, :], window_strides=(1,),
                                 padding=[((CONV_K - 1) // 2, CONV_K // 2)],
                                 dimension_numbers=('NWC', 'WIO', 'NWC'), feature_group_count=qkv.shape[-1])
    y = jax.nn.silu(y)
    aq = _l2(y[..., :512].reshape(b, s, 4, 128))
    ak = _l2(y[..., 512:1024].reshape(b, s, 4, 128))
    av = y[..., 1024:].reshape(b, s, 4, 128)
    beta = jax.nn.sigmoid(small[..., :8]).reshape(b, s, 2, 4)
    ga = -jnp.exp(a_log) * jax.nn.softplus(small[..., 8:16].reshape(b, s, 2, 4) + dt_bias)
    o = (_gdn_jax(aq, ak, av, beta[:, :, 0], ga[:, :, 0])
         + _flip(_gdn_jax(_flip(aq), _flip(ak), _flip(av), _flip(beta[:, :, 1]), _flip(ga[:, :, 1]))))
    return (_rms(o, gdn_norm_g).reshape(b, s, 512) * jax.nn.silu(col(CB_AZ, 4))).astype(BF16)


def _mixer_b_jax(proj, lb, hgrn_norm_g):
    b, s, _ = proj.shape
    col = lambda cb, n: proj[:, :, cb * LANES:(cb + n) * LANES].astype(F32)
    bq = jax.nn.silu(col(CB_BQ, 4)).reshape(b, s, 4, 128)
    bi = col(CB_BI, 4).reshape(b, s, 4, 128)
    bf = col(CB_BF, 8).reshape(b, s, 2, 4, 128)
    kk = (1.0 - lb.reshape(2, 4, 128)) * jax.nn.sigmoid(-bf)
    log_f = jnp.log1p(-jnp.minimum(kk, 1.0 - F_MIN_GAP))
    o = (_hgrn_jax(bq, kk[:, :, 0], bi, log_f[:, :, 0])
         + _flip(_hgrn_jax(_flip(bq), _flip(kk[:, :, 1]), _flip(bi), _flip(log_f[:, :, 1]))))
    return (_rms(o, hgrn_norm_g).reshape(b, s, 512) * jax.nn.silu(col(CB_BZ, 4))).astype(BF16)


def _rope_tables(seq):
    half = ROPE_DIM // 2
    inv = 1.0 / (ROPE_THETA ** (jnp.arange(0, ROPE_DIM, 2, dtype=F32) / ROPE_DIM))
    ang = jnp.arange(seq, dtype=F32)[:, None] * inv[None, :]
    cos, sin = jnp.cos(ang), jnp.sin(ang)
    one = jnp.ones((seq, D_C - ROPE_DIM), F32)
    zero = jnp.zeros((seq, D_C - ROPE_DIM), F32)
    zh = jnp.zeros((seq, half), F32)
    c_map = jnp.concatenate([cos, cos, one], axis=1)
    s1_map = jnp.concatenate([zh, sin, zero], axis=1)
    s2_map = jnp.concatenate([-sin, zh, zero], axis=1)
    tile2 = lambda t: jnp.concatenate([t, t], axis=1)
    return tile2(c_map), tile2(s1_map), tile2(s2_map)


def _lower_bounds(lb_logits):
    p = jax.nn.softmax(lb_logits.astype(F32), axis=1)
    return jnp.cumsum(p, axis=1) - p[:, :1]


def kernel(x_prompt, x_sample, norm_g, w_in, conv_w, a_log, dt_bias, gdn_norm_g, hgrn_lb_logits, hgrn_norm_g,
           q_norm_g, k_norm_g, diff_lambda, subln_g, w_br_a, w_br_b, w_br_c, w_out):
    nb_p = x_prompt.shape[0]
    x = jnp.concatenate([x_prompt, x_sample], axis=0)
    b, s, d = x.shape
    depth = w_in.shape[0]
    n_in = w_in.shape[-1]
    gate0 = n_in - 3 * D_MODEL
    w_main = jnp.concatenate([w_in[:, :, gate0:], w_in[:, :, :2048], w_in[:, :, 2048 + N_SMALL:gate0]],
                             axis=-1).astype(BF16)
    w_small = jnp.pad(w_in[:, :, 2048:2048 + N_SMALL], ((0, 0), (0, 0), (0, LANES - N_SMALL))).astype(BF16)
    wa, wb, wc, wo = (w.astype(BF16) for w in (w_br_a, w_br_b, w_br_c, w_out))
    cos_t, s1_t, s2_t = _rope_tables(s)
    lbs = _lower_bounds(hgrn_lb_logits)
    lane_pad = ((0, 0), (2 * N_HEADS, LANES - 4 * N_HEADS))
    neg_a_vec = jnp.pad(-jnp.exp(a_log.astype(F32)).reshape(depth, 2 * N_HEADS), lane_pad)[:, None, :]
    dtb_vec = jnp.pad(dt_bias.astype(F32).reshape(depth, 2 * N_HEADS), lane_pad)[:, None, :]
    lp = diff_lambda.astype(F32)
    lam_dyn = jnp.exp(jnp.sum(lp[:, 0] * lp[:, 1], axis=-1)) - jnp.exp(jnp.sum(lp[:, 2] * lp[:, 3], axis=-1))

    x2d = x.reshape(b * s, d)
    for l in range(depth):
        lambda_init = 0.8 - 0.6 * math.exp(-0.3 * l)
        proj2d, small2d = _inproj(x2d, norm_g[l][None, :], w_main[l], w_small[l])
        proj = proj2d.reshape(b, s, N_MAIN)
        small = small2d.reshape(b, s, LANES)
        small_t = small[:, :, :N_SMALL].transpose(0, 2, 1).reshape(b, N_SMALL, s // CHUNK_A, CHUNK_A)
        ya = _mixer_a(proj, small, small_t, conv_w[l], a_log[l], dt_bias[l], neg_a_vec[l], dtb_vec[l],
                      gdn_norm_g[l][None, :])
        yb = _mixer_b(proj, lbs[:, l], hgrn_norm_g[l][None, :])
        qn, kn = _qkprep(proj, cos_t, s1_t, s2_t, q_norm_g[l].reshape(1, LANES), k_norm_g[l].reshape(1, LANES))
        scal = jnp.stack([lam_dyn[l] + lambda_init, jnp.asarray(1.0 - lambda_init, F32)]).astype(F32)
        yc = _attn(scal, qn, kn, proj, subln_g[l][None, :])
        x2d = _merge(x2d, ya.reshape(b * s, -1), yb.reshape(b * s, -1), yc.reshape(b * s, -1), proj2d,
                     wa[l], wb[l], wc[l], wo[l])
    y = x2d.reshape(b, s, d)
    return (y[:nb_p], y[nb_p:])
```

```python
import functools
import math

import jax
import jax.numpy as jnp
from jax import lax
from jax.experimental import pallas as pl
from jax.experimental.pallas import tpu as pltpu

F32 = jnp.float32
BF16 = jnp.bfloat16
EPS = 1e-6
F_MIN_GAP = 1e-6
LOG2E = math.log2(math.e)

D_MODEL = 1024
LANES = 128
SUB = 8
N_HEADS = 4
HEAD_W = 128
CONV_K = 5
CHUNK_A = 64
CHUNK_B = 16
D_C = 64
ROPE_DIM = D_C // 4
ROPE_THETA = 500000.0
N_SMALL = 16
N_MAIN = 9728
CB_AQ, CB_AK, CB_AV, CB_AZ = 24, 28, 32, 36
CB_BQ, CB_BI, CB_BF, CB_BZ = 40, 44, 48, 56
CB_CQ, CB_CK, CB_CV, CB_CZ = 60, 64, 68, 72
VMEM_LIMIT = 56 * 1024 * 1024


def _cparams(sem):
    return pltpu.CompilerParams(dimension_semantics=sem, vmem_limit_bytes=VMEM_LIMIT)


def _inproj_body(x_ref, g_ref, w_ref, ws_ref, o_ref, os_ref, h_ref):
    @pl.when(pl.program_id(1) == 0)
    def _():
        x = x_ref[...]
        h = x * lax.rsqrt(jnp.mean(x * x, axis=-1, keepdims=True) + EPS) * g_ref[...]
        hb = h.astype(BF16)
        h_ref[...] = hb
        os_ref[...] = jnp.dot(hb, ws_ref[...], preferred_element_type=F32)

    o_ref[...] = jnp.dot(h_ref[...], w_ref[...], preferred_element_type=F32).astype(BF16)


def _inproj(x2d, g, w_main, w_small, tm=1024, tn=2432):
    t = x2d.shape[0]
    while t % tm:
        tm //= 2
    return pl.pallas_call(
        _inproj_body,
        grid=(t // tm, N_MAIN // tn),
        in_specs=[
            pl.BlockSpec((tm, D_MODEL), lambda i, j: (i, 0)),
            pl.BlockSpec((1, D_MODEL), lambda i, j: (0, 0)),
            pl.BlockSpec((D_MODEL, tn), lambda i, j: (0, j)),
            pl.BlockSpec((D_MODEL, LANES), lambda i, j: (0, 0)),
        ],
        out_specs=[
            pl.BlockSpec((tm, tn), lambda i, j: (i, j)),
            pl.BlockSpec((tm, LANES), lambda i, j: (i, 0)),
        ],
        out_shape=[
            jax.ShapeDtypeStruct((t, N_MAIN), BF16),
            jax.ShapeDtypeStruct((t, LANES), F32),
        ],
        scratch_shapes=[pltpu.VMEM((tm, D_MODEL), BF16)],
        compiler_params=_cparams(("parallel", "arbitrary")),
        name="inproj",
    )(x2d, g, w_main, w_small)


def _qkprep_body(q_ref, k_ref, c_ref, s1_ref, s2_ref, qg_ref, kg_ref, qo_ref, ko_ref, *, q_scale):
    lane = lax.broadcasted_iota(jnp.int32, (1, LANES), 1)
    cos, s1, s2 = c_ref[...], s1_ref[...], s2_ref[...]

    def prep(x, g):
        x = x.astype(F32)
        x2 = x * x
        lo = jnp.sum(x2[:, :D_C], axis=-1, keepdims=True)
        hi = jnp.sum(x2[:, D_C:], axis=-1, keepdims=True)
        ms = jnp.where(lane < D_C, lo, hi) * (1.0 / D_C)
        y = x * lax.rsqrt(ms + EPS) * g
        return y * cos + pltpu.roll(y, ROPE_DIM // 2, 1) * s1 + pltpu.roll(y, LANES - ROPE_DIM // 2, 1) * s2

    qo_ref[0] = (prep(q_ref[0], qg_ref[...]) * q_scale).astype(BF16)
    ko_ref[0] = prep(k_ref[0], kg_ref[...]).astype(BF16)


def _qkprep(proj, cos_t, s1_t, s2_t, qg, kg, ts=1024):
    b, s, _ = proj.shape
    ts = min(ts, s)
    q_scale = (D_C ** -0.5) * math.log2(math.e)
    tab = pl.BlockSpec((ts, LANES), lambda bi, si, h: (si, 0))
    vec = pl.BlockSpec((1, LANES), lambda bi, si, h: (0, 0))
    out = pl.BlockSpec((1, ts, LANES), lambda bi, si, h: (bi, si, h))
    return pl.pallas_call(
        functools.partial(_qkprep_body, q_scale=q_scale),
        grid=(b, s // ts, N_HEADS),
        in_specs=[
            pl.BlockSpec((1, ts, LANES), lambda bi, si, h: (bi, si, CB_CQ + h)),
            pl.BlockSpec((1, ts, LANES), lambda bi, si, h: (bi, si, CB_CK + h)),
            tab, tab, tab, vec, vec,
        ],
        out_specs=[out, out],
        out_shape=[jax.ShapeDtypeStruct((b, s, N_HEADS * HEAD_W), BF16)] * 2,
        compiler_params=_cparams(("parallel", "parallel", "parallel")),
        name="qkprep",
    )(proj, proj, cos_t, s1_t, s2_t, qg, kg)


ATTN_SUB = 256


def _attn_body(sc_ref, q_ref, k_ref, v_ref, z_ref, g_ref, o_ref):
    lam = sc_ref[0]
    out_scale = sc_ref[1]
    k = k_ref[0]
    v = v_ref[0]
    lane = lax.broadcasted_iota(jnp.int32, (1, LANES), 1)
    n_sub = q_ref.shape[1] // ATTN_SUB

    def scores(r, m):
        q = q_ref[0, r * ATTN_SUB:(r + 1) * ATTN_SUB, :]
        qm = jnp.where((lane < D_C) if m == 0 else (lane >= D_C), q, jnp.zeros_like(q))
        return _dot_nt(qm, k)

    def weighted(s):
        p = jnp.exp2(s - jnp.max(s, axis=-1, keepdims=True))
        return _dot(p.astype(BF16), v) / jnp.sum(p, axis=-1, keepdims=True)

    units = [(r, m) for r in range(n_sub) for m in range(2)]
    outs = {}
    s_next = scores(*units[0])
    for idx, unit in enumerate(units):
        s_cur = s_next
        if idx + 1 < len(units):
            s_next = scores(*units[idx + 1])
        outs[unit] = weighted(s_cur)
    for r in range(n_sub):
        rows = slice(r * ATTN_SUB, (r + 1) * ATTN_SUB)
        o = outs[(r, 0)] - lam * outs[(r, 1)]
        y = o * lax.rsqrt(jnp.mean(o * o, axis=-1, keepdims=True) + EPS) * g_ref[...]
        z = z_ref[0, rows, :].astype(F32)
        o_ref[0, rows, :] = (y * out_scale * (z * jax.nn.sigmoid(z))).astype(BF16)


def _attn(scal, qn, kn, proj, subln_g, tq=1024):
    b, s, _ = proj.shape
    tq = min(tq, s)
    assert tq % ATTN_SUB == 0 and s % tq == 0
    return pl.pallas_call(
        _attn_body,
        grid=(b, N_HEADS, s // tq),
        in_specs=[
            pl.BlockSpec(memory_space=pltpu.SMEM),
            pl.BlockSpec((1, tq, LANES), lambda bi, h, qi: (bi, qi, h)),
            pl.BlockSpec((1, s, LANES), lambda bi, h, qi: (bi, 0, h)),
            pl.BlockSpec((1, s, LANES), lambda bi, h, qi: (bi, 0, CB_CV + h)),
            pl.BlockSpec((1, tq, LANES), lambda bi, h, qi: (bi, qi, CB_CZ + h)),
            pl.BlockSpec((1, LANES), lambda bi, h, qi: (0, 0)),
        ],
        out_specs=pl.BlockSpec((1, tq, LANES), lambda bi, h, qi: (bi, qi, h)),
        out_shape=jax.ShapeDtypeStruct((b, s, N_HEADS * HEAD_W), BF16),
        compiler_params=_cparams(("parallel", "parallel", "arbitrary")),
        name="diffattn",
    )(scal, qn, kn, proj, proj, subln_g)


def _merge_body(x_ref, ya_ref, yb_ref, yc_ref, g0_ref, g1_ref, g2_ref, wa_ref, wb_ref, wc_ref, wo_ref, o_ref):
    def branch(y_ref, w_ref, g_ref):
        return jax.nn.sigmoid(g_ref[...].astype(F32)) * jnp.dot(y_ref[...], w_ref[...], preferred_element_type=F32)

    merged = branch(ya_ref, wa_ref, g0_ref) + branch(yb_ref, wb_ref, g1_ref) + branch(yc_ref, wc_ref, g2_ref)
    o_ref[...] = x_ref[...] + jnp.dot(merged.astype(BF16), wo_ref[...], preferred_element_type=F32)


def _merge(x2d, ya, yb, yc, proj2d, wa, wb, wc, wo, tm=512):
    t = x2d.shape[0]
    w_br = N_HEADS * HEAD_W
    row = lambda i: (i, 0)
    full = lambda i: (0, 0)
    y_spec = pl.BlockSpec((tm, w_br), row)
    wbr_spec = pl.BlockSpec((w_br, D_MODEL), full)
    return pl.pallas_call(
        _merge_body,
        grid=(t // tm,),
        in_specs=[
            pl.BlockSpec((tm, D_MODEL), row),
            y_spec, y_spec, y_spec,
            pl.BlockSpec((tm, D_MODEL), lambda i: (i, 0)),
            pl.BlockSpec((tm, D_MODEL), lambda i: (i, 1)),
            pl.BlockSpec((tm, D_MODEL), lambda i: (i, 2)),
            wbr_spec, wbr_spec, wbr_spec,
            pl.BlockSpec((D_MODEL, D_MODEL), full),
        ],
        out_specs=pl.BlockSpec((tm, D_MODEL), row),
        out_shape=jax.ShapeDtypeStruct((t, D_MODEL), F32),
        compiler_params=_cparams(("parallel",)),
        name="merge",
    )(x2d, ya, yb, yc, proj2d, proj2d, proj2d, wa, wb, wc, wo)


def _split3(x):
    p1 = x.astype(BF16)
    r1 = x - p1.astype(F32)
    p2 = r1.astype(BF16)
    p3 = (r1 - p2.astype(F32)).astype(BF16)
    return p1, p2, p3


def _softplus(x):
    return jnp.maximum(x, 0.0) + jnp.log1p(jnp.exp(-jnp.abs(x)))


def _silu(x):
    return x * jax.nn.sigmoid(x)


def _dot(a, b):
    return jnp.dot(a, b, preferred_element_type=F32)


def _dot_nt(a, b):
    return lax.dot_general(a, b, (((1,), (1,)), ((), ())), preferred_element_type=F32)


CUM_TILE = 256
GDN_GROUP = 4


def _gdn_body(alog_ref, dtb_ref, q_ref, k_ref, v_ref, z_ref, cwq_ref, cwk_ref, cwv_ref, sm_ref, smt_ref,
              nav_ref, dtv_ref, ng_ref, o_ref,
              xp_ref, qn_ref, kn_ref, vn_ref, gcc_ref, bc_ref, gcr_ref, br_ref,
              lhs_ref, r_ref, gl_ref, of_ref, ob_ref):
    h = pl.program_id(1)
    seq = q_ref.shape[1]
    c = CHUNK_A
    n_chunks = seq // c
    dk = HEAD_W

    row_t = lax.broadcasted_iota(jnp.int32, (CUM_TILE, CUM_TILE), 0)
    col_t = lax.broadcasted_iota(jnp.int32, (CUM_TILE, CUM_TILE), 1)
    same_chunk = (row_t // c) == (col_t // c)
    tri = (jnp.where(same_chunk & (col_t <= row_t), 1.0, 0.0).astype(BF16),
           jnp.where(same_chunk & (col_t >= row_t), 1.0, 0.0).astype(BF16))
    lane_row = lax.broadcasted_iota(jnp.int32, (1, LANES), 1)
    sel_src = lax.broadcasted_iota(jnp.int32, (LANES, 4 * LANES), 0)
    sel_blk = lax.broadcasted_iota(jnp.int32, (LANES, 4 * LANES), 1) // LANES
    sel_want = jnp.where(sel_blk < 2, 2 * N_HEADS + N_HEADS * sel_blk, N_HEADS * (sel_blk - 2)) + h
    sel4 = jnp.where(sel_src == sel_want, 1.0, 0.0).astype(BF16)

    def gates_tile(t, carry):
        rows = pl.ds(pl.multiple_of(t * CUM_TILE, CUM_TILE), CUM_TILE)
        sm = sm_ref[0, rows, :]
        gates = jnp.where(lane_row < 2 * N_HEADS, jax.nn.sigmoid(sm), nav_ref[...] * _softplus(sm + dtv_ref[...]))
        rep = _dot(jnp.concatenate(_split3(gates), axis=0), sel4)
        piece = lambda p, blk: rep[p * CUM_TILE:(p + 1) * CUM_TILE, blk * LANES:(blk + 1) * LANES]
        for d in range(2):
            cs = _dot(tri[d], jnp.concatenate([piece(p, d) for p in range(3)], axis=1).astype(BF16))
            gcc_ref[d, rows, :] = cs[:, :LANES] + cs[:, LANES:2 * LANES] + cs[:, 2 * LANES:]
            bc_ref[d, rows, :] = piece(0, 2 + d) + piece(1, 2 + d) + piece(2, 2 + d)
        return carry

    lax.fori_loop(0, seq // CUM_TILE, gates_tile, 0)

    rc = lax.broadcasted_iota(jnp.int32, (c, c), 0)
    cc = lax.broadcasted_iota(jnp.int32, (c, c), 1)
    for d in range(2):
        a_neg = -jnp.exp(alog_ref[d, h])
        g = a_neg * _softplus(smt_ref[0, 8 + 4 * d + h] + dtb_ref[d, h])
        cum = jnp.where((rc <= cc) if d == 0 else (rc >= cc), 1.0, 0.0).astype(BF16)
        acc = jnp.zeros((n_chunks, c), F32)
        for part in _split3(g):
            acc = acc + _dot(part, cum)
        gcr_ref[d] = acc
        br_ref[d] = jax.nn.sigmoid(smt_ref[0, 4 * d + h])

    pad = 8
    zeros_pad = jnp.zeros((pad, LANES), F32)

    def conv_into(src_ref, cw_ref, dst_ref, normalise, scale):
        xp_ref[pl.ds(0, pad), :] = zeros_pad
        xp_ref[pl.ds(pad + seq, pad), :] = zeros_pad
        xp_ref[pl.ds(pad, seq), :] = src_ref[0].astype(F32)
        cw = cw_ref[...]

        def chunk(n, carry):
            base = pl.multiple_of(n * c, c)
            win = xp_ref[pl.ds(base, c + 2 * pad), :]
            y = jnp.zeros((c, LANES), F32)
            for j in range(CONV_K):
                off = pad - (CONV_K - 1) // 2 + j
                y = y + win[off:off + c, :] * cw[j:j + 1, :]
            y = _silu(y)
            if normalise:
                y = y * lax.rsqrt(jnp.sum(y * y, axis=-1, keepdims=True) + EPS)
            dst_ref[pl.ds(base, c), :] = y * scale if scale != 1.0 else y
            return carry

        lax.fori_loop(0, n_chunks, chunk, 0, unroll=4)

    conv_into(q_ref, cwq_ref, qn_ref, True, dk ** -0.5)
    conv_into(k_ref, cwk_ref, kn_ref, True, 1.0)
    conv_into(v_ref, cwv_ref, vn_ref, False, 1.0)

    eye = jnp.where(rc == cc, 1.0, 0.0)
    masks = (((rc >= cc), (rc > cc)), ((rc <= cc), (rc < cc)))

    def prep_group(i, carry):
        chains = []
        for g in range(GDN_GROUP):
            n = i * GDN_GROUP + g
            rows = pl.ds(pl.multiple_of(n * c, c), c)
            q = qn_ref[rows, :]
            k = kn_ref[rows, :]
            k16 = k.astype(BF16)
            v16 = vn_ref[rows, :].astype(BF16)
            kq = _dot_nt(jnp.concatenate([k16, q.astype(BF16)], axis=0), k16)
            kk, qk = kq[:c, :], kq[c:, :]
            for d in range(2):
                incl, strict = masks[d]
                gcol = gcc_ref[d, rows, :]
                grow = gcr_ref[d, pl.ds(n, 1), :]
                decay = jnp.where(incl, jnp.exp(jnp.where(incl, gcol[:, :c] - grow, 0.0)), 0.0)
                x = -jnp.where(strict, kk * decay * bc_ref[d, rows, :][:, :c], 0.0)
                last = (c - 1) if d == 0 else 0
                g_last = gcol[last:last + 1, :]
                gl_ref[d, pl.ds(n, 1), :] = jnp.exp(g_last)
                chains.append(dict(d=d, n=n, rows=rows, x=x, grow=grow, k16=k16, v16=v16,
                                   a16=(qk * decay).astype(BF16), qd=q * jnp.exp(gcol),
                                   kdt16=(k * jnp.exp(g_last - gcol)).T.astype(BF16)))
        t_inv = [eye + ch["x"] for ch in chains]
        pw16 = [ch["x"].astype(BF16) for ch in chains]
        pw16 = [_dot(p, p).astype(BF16) for p in pw16]
        for _ in range(4):
            both = [_dot(jnp.concatenate([t.astype(BF16), p], axis=0), p) for t, p in zip(t_inv, pw16)]
            t_inv = [t + r[:c, :] for t, r in zip(t_inv, both)]
            pw16 = [r[c:, :].astype(BF16) for r in both]
        t_inv = [t + _dot(t.astype(BF16), p) for t, p in zip(t_inv, pw16)]
        wu = []
        for ch, t in zip(chains, t_inv):
            tb = t * br_ref[ch["d"], pl.ds(ch["n"], 1), :]
            res = _dot(jnp.concatenate([(tb * jnp.exp(ch["grow"])).astype(BF16), tb.astype(BF16)], axis=0),
                       jnp.concatenate([ch["k16"], ch["v16"]], axis=1))
            wu.append(jnp.concatenate([res[:c, :dk], res[c:, dk:]], axis=1).astype(BF16))
        for ch, wu16 in zip(chains, wu):
            d, n = ch["d"], ch["n"]
            res = _dot(jnp.concatenate([ch["kdt16"], ch["a16"]], axis=0), wu16)
            lhs_ref[d, n, pl.ds(0, dk), :] = res[:dk, :dk].astype(BF16)
            lhs_ref[d, n, pl.ds(dk, c), :] = (ch["qd"] - res[dk:, :dk]).astype(BF16)
            r_ref[d, n] = res[:dk, dk:].astype(BF16)
            (of_ref if d == 0 else ob_ref)[ch["rows"], :] = res[dk:, dk:]
        return carry

    lax.fori_loop(0, n_chunks // GDN_GROUP, prep_group, 0)

    def scan_step(i, carry):
        outs = []
        for d, s, o_out in ((0, carry[0], of_ref), (1, carry[1], ob_ref)):
            n = i if d == 0 else n_chunks - 1 - i
            rows = pl.ds(pl.multiple_of(n * c, c), c)
            res = _dot(lhs_ref[d, n], s.astype(BF16))
            o_out[rows, :] = o_out[rows, :] + res[dk:, :]
            outs.append(s * gl_ref[d, pl.ds(n, 1), :] + r_ref[d, n].astype(F32) - res[:dk, :])
        return tuple(outs)

    s0 = jnp.zeros((dk, HEAD_W), F32)
    lax.fori_loop(0, n_chunks, scan_step, (s0, s0))

    def finish(t, carry):
        rows = pl.ds(pl.multiple_of(t * CUM_TILE, CUM_TILE), CUM_TILE)
        o = of_ref[rows, :] + ob_ref[rows, :]
        y = o * lax.rsqrt(jnp.mean(o * o, axis=-1, keepdims=True) + EPS) * ng_ref[...]
        o_ref[0, rows, :] = (y * _silu(z_ref[0, rows, :].astype(F32))).astype(BF16)
        return carry

    lax.fori_loop(0, seq // CUM_TILE, finish, 0)


def _mixer_a(proj, small, small_t, conv_w, a_log, dt_bias, neg_a_vec, dtb_vec, norm_g):
    b, s, _ = proj.shape
    n_chunks = s // CHUNK_A
    col = lambda cb: pl.BlockSpec((1, s, LANES), lambda bi, h: (bi, 0, cb + h))
    cw = lambda cb: pl.BlockSpec((CONV_K, LANES), lambda bi, h: (0, cb + h))
    vec = pl.BlockSpec((1, LANES), lambda bi, h: (0, 0))
    smem = pl.BlockSpec(memory_space=pltpu.SMEM)
    return pl.pallas_call(
        _gdn_body,
        grid=(b, N_HEADS),
        in_specs=[
            smem, smem,
            col(CB_AQ), col(CB_AK), col(CB_AV), col(CB_AZ),
            cw(0), cw(4), cw(8),
            pl.BlockSpec((1, s, LANES), lambda bi, h: (bi, 0, 0)),
            pl.BlockSpec((1, N_SMALL, n_chunks, CHUNK_A), lambda bi, h: (bi, 0, 0, 0)),
            vec, vec, vec,
        ],
        out_specs=pl.BlockSpec((1, s, LANES), lambda bi, h: (bi, 0, h)),
        out_shape=jax.ShapeDtypeStruct((b, s, N_HEADS * HEAD_W), BF16),
        scratch_shapes=[
            pltpu.VMEM((s + 16, LANES), F32),
            pltpu.VMEM((s, LANES), F32),
            pltpu.VMEM((s, LANES), F32),
            pltpu.VMEM((s, LANES), F32),
            pltpu.VMEM((2, s, LANES), F32),
            pltpu.VMEM((2, s, LANES), F32),
            pltpu.VMEM((2, n_chunks, CHUNK_A), F32),
            pltpu.VMEM((2, n_chunks, CHUNK_A), F32),
            pltpu.VMEM((2, n_chunks, HEAD_W + CHUNK_A, LANES), BF16),
            pltpu.VMEM((2, n_chunks, HEAD_W, LANES), BF16),
            pltpu.VMEM((2, n_chunks, LANES), F32),
            pltpu.VMEM((s, LANES), F32),
            pltpu.VMEM((s, LANES), F32),
        ],
        compiler_params=_cparams(("parallel", "arbitrary")),
        name="gdn",
    )(a_log, dt_bias, proj, proj, proj, proj, conv_w, conv_w, conv_w, small, small_t, neg_a_vec, dtb_vec, norm_g)


HGRN_GROUP = 4


def _hgrn_body(q_ref, i_ref, ff_ref, fb_ref, z_ref, lb_ref, ng_ref, o_ref,
               qs_ref, kk_ref, gc_ref, gt_ref, qd_ref, kd_ref, of_ref, ob_ref):
    seq = q_ref.shape[1]
    c = CHUNK_B
    n_chunks = seq // c
    dk = HEAD_W

    row_t = lax.broadcasted_iota(jnp.int32, (CUM_TILE, CUM_TILE), 0)
    col_t = lax.broadcasted_iota(jnp.int32, (CUM_TILE, CUM_TILE), 1)
    same_chunk = (row_t // c) == (col_t // c)
    tri = (jnp.where(same_chunk & (col_t <= row_t), 1.0, 0.0).astype(BF16),
           jnp.where(same_chunk & (col_t >= row_t), 1.0, 0.0).astype(BF16))
    tot = jnp.where(same_chunk, 1.0, 0.0).astype(BF16)

    def gates_tile(t, carry):
        rows = pl.ds(pl.multiple_of(t * CUM_TILE, CUM_TILE), CUM_TILE)
        qs = _silu(q_ref[0, rows, :].astype(F32)) * (dk ** -0.5)
        qs_ref[rows, :] = qs
        for d, f_ref in ((0, ff_ref), (1, fb_ref)):
            bf = f_ref[0, rows, :].astype(F32)
            kk = (1.0 - lb_ref[d:d + 1, :]) * jax.nn.sigmoid(-bf)
            log_f = jnp.log1p(-jnp.minimum(kk, 1.0 - F_MIN_GAP))
            gc = jnp.zeros((CUM_TILE, LANES), F32)
            gt = jnp.zeros((CUM_TILE, LANES), F32)
            for part in _split3(log_f):
                gc = gc + _dot(tri[d], part)
                gt = gt + _dot(tot, part)
            kk_ref[d, rows, :] = kk
            gc_ref[d, rows, :] = gc * LOG2E
            gt_ref[d, rows, :] = gt
            qd_ref[d, rows, :] = (qs * jnp.exp(gc)).astype(BF16)
            kd_ref[d, rows, :] = (kk * jnp.exp(gt - gc)).astype(BF16)
        return carry

    lax.fori_loop(0, seq // CUM_TILE, gates_tile, 0)

    jrow = lax.broadcasted_iota(jnp.int32, (SUB, LANES), 0)
    ones_w = jnp.ones((LANES, LANES), BF16)
    n_pieces = c + c // 2
    piece = lax.broadcasted_iota(jnp.int32, (c, n_pieces * SUB), 1) // SUB
    out_row = lax.broadcasted_iota(jnp.int32, (c, n_pieces * SUB), 0)
    row_of_piece = (jnp.where(piece < SUB, piece, SUB + (piece - SUB) // 2),
                    jnp.where(piece < c, piece // 2, piece - SUB))
    sel = tuple(jnp.where(rp == out_row, 1.0, 0.0).astype(BF16) for rp in row_of_piece)

    def scan_step(i, carry):
        work = []
        for g in range(HGRN_GROUP):
            for d in range(2):
                n = i * HGRN_GROUP + g
                n = n if d == 0 else n_chunks - 1 - n
                work.append((d, pl.ds(pl.multiple_of(n * c, c), c)))
        tiles, vs, upds, es = [], [], [], []
        for d, rows in work:
            qs = qs_ref[rows, :]
            kk = kk_ref[d, rows, :]
            gc2 = gc_ref[d, rows, :]
            v16 = i_ref[0, rows, :]
            v = v16.astype(F32)
            parts, vparts = [], []
            for r in range(c):
                for j0 in (0, SUB):
                    if (j0 > r) if d == 0 else (j0 + SUB - 1 < r):
                        continue
                    pair = jnp.exp2(gc2[r:r + 1, :] - gc2[j0:j0 + SUB, :])
                    if not ((j0 + SUB - 1 <= r) if d == 0 else (j0 >= r)):
                        mask = (jrow + j0 <= r) if d == 0 else (jrow + j0 >= r)
                        pair = jnp.where(mask, pair, 0.0)
                    parts.append(pair * kk[j0:j0 + SUB, :] * qs[r:r + 1, :])
                    vparts.append(v[j0:j0 + SUB, :])
            tiles.append(jnp.concatenate(parts, axis=0).astype(BF16))
            vs.append(jnp.concatenate(vparts, axis=0))
            upds.append(lax.dot_general(v16, kd_ref[d, rows, :], (((0,), (0,)), ((), ())),
                                        preferred_element_type=F32))
            es.append(jnp.exp(gt_ref[d, pl.ds(rows.start, 1), :]))
        a_reps = [_dot(t, ones_w) for t in tiles]
        ps = [(a * v).astype(BF16) for a, v in zip(a_reps, vs)]
        o_intra = [_dot(sel[d], p) for (d, _), p in zip(work, ps)]
        st = list(carry)
        st16 = []
        for (d, rows), upd, e_row in zip(work, upds, es):
            st16.append(st[d].astype(BF16))
            st[d] = st[d] * e_row + upd
        for (d, rows), oi, s16 in zip(work, o_intra, st16):
            o_out = of_ref if d == 0 else ob_ref
            o_out[rows, :] = oi + _dot_nt(qd_ref[d, rows, :], s16)
        return tuple(st)

    s0 = jnp.zeros((HEAD_W, dk), F32)
    lax.fori_loop(0, n_chunks // HGRN_GROUP, scan_step, (s0, s0))

    def finish(t, carry):
        rows = pl.ds(pl.multiple_of(t * CUM_TILE, CUM_TILE), CUM_TILE)
        o = of_ref[rows, :] + ob_ref[rows, :]
        y = o * lax.rsqrt(jnp.mean(o * o, axis=-1, keepdims=True) + EPS) * ng_ref[...]
        o_ref[0, rows, :] = (y * _silu(z_ref[0, rows, :].astype(F32))).astype(BF16)
        return carry

    lax.fori_loop(0, seq // CUM_TILE, finish, 0)


def _mixer_b(proj, lb, norm_g):
    b, s, _ = proj.shape
    col = lambda cb: pl.BlockSpec((1, s, LANES), lambda bi, h: (bi, 0, cb + h))
    return pl.pallas_call(
        _hgrn_body,
        grid=(b, N_HEADS),
        in_specs=[
            col(CB_BQ), col(CB_BI), col(CB_BF), col(CB_BF + N_HEADS), col(CB_BZ),
            pl.BlockSpec((2, LANES), lambda bi, h: (0, h)),
            pl.BlockSpec((1, LANES), lambda bi, h: (0, 0)),
        ],
        out_specs=pl.BlockSpec((1, s, LANES), lambda bi, h: (bi, 0, h)),
        out_shape=jax.ShapeDtypeStruct((b, s, N_HEADS * HEAD_W), BF16),
        scratch_shapes=[
            pltpu.VMEM((s, LANES), F32),
            pltpu.VMEM((2, s, LANES), F32),
            pltpu.VMEM((2, s, LANES), F32),
            pltpu.VMEM((2, s, LANES), F32),
            pltpu.VMEM((2, s, LANES), BF16),
            pltpu.VMEM((2, s, LANES), BF16),
            pltpu.VMEM((s, LANES), F32),
            pltpu.VMEM((s, LANES), F32),
        ],
        compiler_params=_cparams(("parallel", "arbitrary")),
        name="hgrn",
    )(proj, proj, proj, proj, proj, lb, norm_g)


def _rms(x, g):
    return x * lax.rsqrt(jnp.mean(x * x, axis=-1, keepdims=True) + EPS) * g


def _l2(x):
    return x * lax.rsqrt(jnp.sum(x * x, axis=-1, keepdims=True) + EPS)


def _to_chunks(t, c):
    b, s, h, d = t.shape
    return t.reshape(b, s // c, c, h, d).transpose(1, 0, 3, 2, 4)


def _from_chunks(t):
    n, b, h, c, d = t.shape
    return t.transpose(1, 0, 3, 2, 4).reshape(b, n * c, h, d)


def _mexp(diff, mask):
    return jnp.where(mask, jnp.exp(jnp.where(mask, diff, 0.0)), 0.0)


def _gdn_jax(q, k, v, beta, g):
    bsz, _, nh, dk = k.shape
    dv = v.shape[-1]
    q, k, v = (_to_chunks(t, CHUNK_A) for t in (q, k, v))
    beta = _to_chunks(beta[..., None], CHUNK_A)[..., 0]
    gc = jnp.cumsum(_to_chunks(g[..., None], CHUNK_A)[..., 0], axis=-1)
    incl = jnp.tril(jnp.ones((CHUNK_A, CHUNK_A), bool))
    strict = jnp.tril(jnp.ones((CHUNK_A, CHUNK_A), bool), -1)
    decay = _mexp(gc[..., :, None] - gc[..., None, :], incl)
    kb = k * beta[..., None]
    m = jnp.where(strict, jnp.einsum('nbhid,nbhjd->nbhij', kb, k) * decay, 0.0)
    lhs = m + jnp.eye(CHUNK_A, dtype=F32)
    rhs = jnp.concatenate([v * beta[..., None], kb * jnp.exp(gc)[..., None]], axis=-1)
    sol = lax.linalg.triangular_solve(lhs, rhs, left_side=True, lower=True)
    u, w = sol[..., :dv], sol[..., dv:]
    qs = q * (dk ** -0.5)
    a_qk = jnp.einsum('nbhid,nbhjd->nbhij', qs, k) * decay
    q_dec = qs * jnp.exp(gc)[..., None]
    g_last = gc[..., -1]
    k_dec = k * jnp.exp(g_last[..., None] - gc)[..., None]

    def step(s, inp):
        u_c, w_c, qd_c, a_c, kd_c, gl_c = inp
        v_new = u_c - jnp.einsum('bhck,bhkv->bhcv', w_c, s)
        o = jnp.einsum('bhck,bhkv->bhcv', qd_c, s) + jnp.einsum('bhij,bhjv->bhiv', a_c, v_new)
        s = s * jnp.exp(gl_c)[..., None, None] + jnp.einsum('bhck,bhcv->bhkv', kd_c, v_new)
        return s, o

    s0 = jnp.zeros((bsz, nh, dk, dv), F32)
    _, o = lax.scan(step, s0, (u, w, q_dec, a_qk, k_dec, g_last))
    return _from_chunks(o)


def _hgrn_jax(q, k, v, log_f):
    bsz, _, nh, dk = q.shape
    dv = v.shape[-1]
    q, k, v, log_f = (_to_chunks(t, CHUNK_B) for t in (q, k, v, log_f))
    gc = jnp.cumsum(log_f, axis=-2)
    qs = q * (dk ** -0.5)
    q_dec = qs * jnp.exp(gc)
    g_last = gc[..., -1, :]
    k_dec = k * jnp.exp(g_last[..., None, :] - gc)
    incl = jnp.tril(jnp.ones((CHUNK_B, CHUNK_B), bool))[:, :, None]

    def step(s, inp):
        qs_c, qd_c, k_c, kd_c, v_c, gc_c, gl_c = inp
        pair = _mexp(gc_c[:, :, :, None, :] - gc_c[:, :, None, :, :], incl)
        a = jnp.einsum('bhik,bhjk,bhijk->bhij', qs_c, k_c, pair)
        o = jnp.einsum('bhck,bhkv->bhcv', qd_c, s) + jnp.einsum('bhij,bhjv->bhiv', a, v_c)
        s = s * jnp.exp(gl_c)[..., None] + jnp.einsum('bhck,bhcv->bhkv', kd_c, v_c)
        return s, o

    s0 = jnp.zeros((bsz, nh, dk, dv), F32)
    _, o = lax.scan(step, s0, (qs, q_dec, k, k_dec, v, gc, g_last))
    return _from_chunks(o)


def _flip(t):
    return jnp.flip(t, axis=1)


def _mixer_a_jax(proj, small, conv_w, a_log, dt_bias, gdn_norm_g):
    b, s, _ = proj.shape
    col = lambda cb, n: proj[:, :, cb * LANES:(cb + n) * LANES].astype(F32)
    qkv = col(CB_AQ, 12)
    y = lax.conv_general_dilated(qkv, conv_w[:, None, :], window_strides=(1,),
                                 padding=[((CONV_K - 1) // 2, CONV_K // 2)],
                                 dimension_numbers=('NWC', 'WIO', 'NWC'), feature_group_count=qkv.shape[-1])
    y = jax.nn.silu(y)
    aq = _l2(y[..., :512].reshape(b, s, 4, 128))
    ak = _l2(y[..., 512:1024].reshape(b, s, 4, 128))
    av = y[..., 1024:].reshape(b, s, 4, 128)
    beta = jax.nn.sigmoid(small[..., :8]).reshape(b, s, 2, 4)
    ga = -jnp.exp(a_log) * jax.nn.softplus(small[..., 8:16].reshape(b, s, 2, 4) + dt_bias)
    o = (_gdn_jax(aq, ak, av, beta[:, :, 0], ga[:, :, 0])
         + _flip(_gdn_jax(_flip(aq), _flip(ak), _flip(av), _flip(beta[:, :, 1]), _flip(ga[:, :, 1]))))
    return (_rms(o, gdn_norm_g).reshape(b, s, 512) * jax.nn.silu(col(CB_AZ, 4))).astype(BF16)


def _mixer_b_jax(proj, lb, hgrn_norm_g):
    b, s, _ = proj.shape
    col = lambda cb, n: proj[:, :, cb * LANES:(cb + n) * LANES].astype(F32)
    bq = jax.nn.silu(col(CB_BQ, 4)).reshape(b, s, 4, 128)
    bi = col(CB_BI, 4).reshape(b, s, 4, 128)
    bf = col(CB_BF, 8).reshape(b, s, 2, 4, 128)
    kk = (1.0 - lb.reshape(2, 4, 128)) * jax.nn.sigmoid(-bf)
    log_f = jnp.log1p(-jnp.minimum(kk, 1.0 - F_MIN_GAP))
    o = (_hgrn_jax(bq, kk[:, :, 0], bi, log_f[:, :, 0])
         + _flip(_hgrn_jax(_flip(bq), _flip(kk[:, :, 1]), _flip(bi), _flip(log_f[:, :, 1]))))
    return (_rms(o, hgrn_norm_g).reshape(b, s, 512) * jax.nn.silu(col(CB_BZ, 4))).astype(BF16)


def _rope_tables(seq):
    half = ROPE_DIM // 2
    inv = 1.0 / (ROPE_THETA ** (jnp.arange(0, ROPE_DIM, 2, dtype=F32) / ROPE_DIM))
    ang = jnp.arange(seq, dtype=F32)[:, None] * inv[None, :]
    cos, sin = jnp.cos(ang), jnp.sin(ang)
    one = jnp.ones((seq, D_C - ROPE_DIM), F32)
    zero = jnp.zeros((seq, D_C - ROPE_DIM), F32)
    zh = jnp.zeros((seq, half), F32)
    c_map = jnp.concatenate([cos, cos, one], axis=1)
    s1_map = jnp.concatenate([zh, sin, zero], axis=1)
    s2_map = jnp.concatenate([-sin, zh, zero], axis=1)
    tile2 = lambda t: jnp.concatenate([t, t], axis=1)
    return tile2(c_map), tile2(s1_map), tile2(s2_map)


def _lower_bounds(lb_logits):
    p = jax.nn.softmax(lb_logits.astype(F32), axis=1)
    return jnp.cumsum(p, axis=1) - p[:, :1]


def kernel(x_prompt, x_sample, norm_g, w_in, conv_w, a_log, dt_bias, gdn_norm_g, hgrn_lb_logits, hgrn_norm_g,
           q_norm_g, k_norm_g, diff_lambda, subln_g, w_br_a, w_br_b, w_br_c, w_out):
    nb_p = x_prompt.shape[0]
    x = jnp.concatenate([x_prompt, x_sample], axis=0)
    b, s, d = x.shape
    depth = w_in.shape[0]
    n_in = w_in.shape[-1]
    gate0 = n_in - 3 * D_MODEL
    w_main = jnp.concatenate([w_in[:, :, gate0:], w_in[:, :, :2048], w_in[:, :, 2048 + N_SMALL:gate0]],
                             axis=-1).astype(BF16)
    w_small = jnp.pad(w_in[:, :, 2048:2048 + N_SMALL], ((0, 0), (0, 0), (0, LANES - N_SMALL))).astype(BF16)
    wa, wb, wc, wo = (w.astype(BF16) for w in (w_br_a, w_br_b, w_br_c, w_out))
    cos_t, s1_t, s2_t = _rope_tables(s)
    lbs = _lower_bounds(hgrn_lb_logits)
    lane_pad = ((0, 0), (2 * N_HEADS, LANES - 4 * N_HEADS))
    neg_a_vec = jnp.pad(-jnp.exp(a_log.astype(F32)).reshape(depth, 2 * N_HEADS), lane_pad)[:, None, :]
    dtb_vec = jnp.pad(dt_bias.astype(F32).reshape(depth, 2 * N_HEADS), lane_pad)[:, None, :]
    lp = diff_lambda.astype(F32)
    lam_dyn = jnp.exp(jnp.sum(lp[:, 0] * lp[:, 1], axis=-1)) - jnp.exp(jnp.sum(lp[:, 2] * lp[:, 3], axis=-1))

    x2d = x.reshape(b * s, d)
    for l in range(depth):
        lambda_init = 0.8 - 0.6 * math.exp(-0.3 * l)
        proj2d, small2d = _inproj(x2d, norm_g[l][None, :], w_main[l], w_small[l])
        proj = proj2d.reshape(b, s, N_MAIN)
        small = small2d.reshape(b, s, LANES)
        small_t = small[:, :, :N_SMALL].transpose(0, 2, 1).reshape(b, N_SMALL, s // CHUNK_A, CHUNK_A)
        ya = _mixer_a(proj, small, small_t, conv_w[l], a_log[l], dt_bias[l], neg_a_vec[l], dtb_vec[l],
                      gdn_norm_g[l][None, :])
        yb = _mixer_b(proj, lbs[:, l], hgrn_norm_g[l][None, :])
        qn, kn = _qkprep(proj, cos_t, s1_t, s2_t, q_norm_g[l].reshape(1, LANES), k_norm_g[l].reshape(1, LANES))
        scal = jnp.stack([lam_dyn[l] + lambda_init, jnp.asarray(1.0 - lambda_init, F32)]).astype(F32)
        yc = _attn(scal, qn, kn, proj, subln_g[l][None, :])
        x2d = _merge(x2d, ya.reshape(b * s, -1), yb.reshape(b * s, -1), yc.reshape(b * s, -1), proj2d,
                     wa[l], wb[l], wc[l], wo[l])
    y = x2d.reshape(b, s, d)
    return (y[:nb_p], y[nb_p:])
```

```python
import functools
import math

import jax
import jax.numpy as jnp
from jax import lax
from jax.experimental import pallas as pl
from jax.experimental.pallas import tpu as pltpu

F32 = jnp.float32
BF16 = jnp.bfloat16
EPS = 1e-6
F_MIN_GAP = 1e-6
LOG2E = math.log2(math.e)

D_MODEL = 1024
LANES = 128
SUB = 8
N_HEADS = 4
HEAD_W = 128
CONV_K = 5
CHUNK_A = 64
CHUNK_B = 16
D_C = 64
ROPE_DIM = D_C // 4
ROPE_THETA = 500000.0
N_SMALL = 16
N_MAIN = 9728
CB_AQ, CB_AK, CB_AV, CB_AZ = 24, 28, 32, 36
CB_BQ, CB_BI, CB_BF, CB_BZ = 40, 44, 48, 56
CB_CQ, CB_CK, CB_CV, CB_CZ = 60, 64, 68, 72
VMEM_LIMIT = 56 * 1024 * 1024


def _cparams(sem):
    return pltpu.CompilerParams(dimension_semantics=sem, vmem_limit_bytes=VMEM_LIMIT)


def _inproj_body(x_ref, g_ref, w_ref, ws_ref, o_ref, os_ref, h_ref):
    @pl.when(pl.program_id(1) == 0)
    def _():
        x = x_ref[...]
        h = x * lax.rsqrt(jnp.mean(x * x, axis=-1, keepdims=True) + EPS) * g_ref[...]
        hb = h.astype(BF16)
        h_ref[...] = hb
        os_ref[...] = jnp.dot(hb, ws_ref[...], preferred_element_type=F32)

    o_ref[...] = jnp.dot(h_ref[...], w_ref[...], preferred_element_type=F32).astype(BF16)


def _inproj(x2d, g, w_main, w_small, tm=1024, tn=2432):
    t = x2d.shape[0]
    while t % tm:
        tm //= 2
    return pl.pallas_call(
        _inproj_body,
        grid=(t // tm, N_MAIN // tn),
        in_specs=[
            pl.BlockSpec((tm, D_MODEL), lambda i, j: (i, 0)),
            pl.BlockSpec((1, D_MODEL), lambda i, j: (0, 0)),
            pl.BlockSpec((D_MODEL, tn), lambda i, j: (0, j)),
            pl.BlockSpec((D_MODEL, LANES), lambda i, j: (0, 0)),
        ],
        out_specs=[
            pl.BlockSpec((tm, tn), lambda i, j: (i, j)),
            pl.BlockSpec((tm, LANES), lambda i, j: (i, 0)),
        ],
        out_shape=[
            jax.ShapeDtypeStruct((t, N_MAIN), BF16),
            jax.ShapeDtypeStruct((t, LANES), F32),
        ],
        scratch_shapes=[pltpu.VMEM((tm, D_MODEL), BF16)],
        compiler_params=_cparams(("parallel", "arbitrary")),
        name="inproj",
    )(x2d, g, w_main, w_small)


def _qkprep_body(q_ref, k_ref, c_ref, s1_ref, s2_ref, qg_ref, kg_ref, qo_ref, ko_ref, *, q_scale):
    cos, s1, s2 = c_ref[...], s1_ref[...], s2_ref[...]
    same_map = jnp.where(lax.broadcasted_iota(jnp.int32, (LANES, LANES), 0) // D_C
                         == lax.broadcasted_iota(jnp.int32, (LANES, LANES), 1) // D_C, 1.0, 0.0).astype(BF16)

    def prep(x, g):
        x = x.astype(F32)
        x2 = x * x
        x2_hi = x2.astype(BF16)
        x2_lo = (x2 - x2_hi.astype(F32)).astype(BF16)
        ms = (_dot(x2_hi, same_map) + _dot(x2_lo, same_map)) * (1.0 / D_C)
        y = x * lax.rsqrt(ms + EPS) * g
        return y * cos + pltpu.roll(y, ROPE_DIM // 2, 1) * s1 + pltpu.roll(y, LANES - ROPE_DIM // 2, 1) * s2

    qo_ref[0] = (prep(q_ref[0], qg_ref[...]) * q_scale).astype(BF16)
    ko_ref[0] = prep(k_ref[0], kg_ref[...]).astype(BF16)


def _qkprep(proj, cos_t, s1_t, s2_t, qg, kg, ts=1024):
    b, s, _ = proj.shape
    ts = min(ts, s)
    q_scale = (D_C ** -0.5) * math.log2(math.e)
    tab = pl.BlockSpec((ts, LANES), lambda bi, si, h: (si, 0))
    vec = pl.BlockSpec((1, LANES), lambda bi, si, h: (0, 0))
    out = pl.BlockSpec((1, ts, LANES), lambda bi, si, h: (bi, si, h))
    return pl.pallas_call(
        functools.partial(_qkprep_body, q_scale=q_scale),
        grid=(b, s // ts, N_HEADS),
        in_specs=[
            pl.BlockSpec((1, ts, LANES), lambda bi, si, h: (bi, si, CB_CQ + h)),
            pl.BlockSpec((1, ts, LANES), lambda bi, si, h: (bi, si, CB_CK + h)),
            tab, tab, tab, vec, vec,
        ],
        out_specs=[out, out],
        out_shape=[jax.ShapeDtypeStruct((b, s, N_HEADS * HEAD_W), BF16)] * 2,
        compiler_params=_cparams(("parallel", "parallel", "parallel")),
        name="qkprep",
    )(proj, proj, cos_t, s1_t, s2_t, qg, kg)


ATTN_SUB = 256


def _attn_body(sc_ref, q_ref, k_ref, v_ref, z_ref, g_ref, o_ref):
    lam = sc_ref[0]
    out_scale = sc_ref[1]
    k = k_ref[0]
    v = v_ref[0]
    lane = lax.broadcasted_iota(jnp.int32, (1, LANES), 1)
    n_sub = q_ref.shape[1] // ATTN_SUB

    def scores(r, m):
        q = q_ref[0, r * ATTN_SUB:(r + 1) * ATTN_SUB, :]
        qm = jnp.where((lane < D_C) if m == 0 else (lane >= D_C), q, jnp.zeros_like(q))
        return _dot_nt(qm, k)

    def weighted(s):
        p = jnp.exp2(s - jnp.max(s, axis=-1, keepdims=True))
        return _dot(p.astype(BF16), v) / jnp.sum(p, axis=-1, keepdims=True)

    units = [(r, m) for r in range(n_sub) for m in range(2)]
    outs = {}
    s_next = scores(*units[0])
    for idx, unit in enumerate(units):
        s_cur = s_next
        if idx + 1 < len(units):
            s_next = scores(*units[idx + 1])
        outs[unit] = weighted(s_cur)
    for r in range(n_sub):
        rows = slice(r * ATTN_SUB, (r + 1) * ATTN_SUB)
        o = outs[(r, 0)] - lam * outs[(r, 1)]
        y = o * lax.rsqrt(jnp.mean(o * o, axis=-1, keepdims=True) + EPS) * g_ref[...]
        z = z_ref[0, rows, :].astype(F32)
        o_ref[0, rows, :] = (y * out_scale * (z * jax.nn.sigmoid(z))).astype(BF16)


def _attn(scal, qn, kn, proj, subln_g, tq=1024):
    b, s, _ = proj.shape
    tq = min(tq, s)
    assert tq % ATTN_SUB == 0 and s % tq == 0
    return pl.pallas_call(
        _attn_body,
        grid=(b, N_HEADS, s // tq),
        in_specs=[
            pl.BlockSpec(memory_space=pltpu.SMEM),
            pl.BlockSpec((1, tq, LANES), lambda bi, h, qi: (bi, qi, h)),
            pl.BlockSpec((1, s, LANES), lambda bi, h, qi: (bi, 0, h)),
            pl.BlockSpec((1, s, LANES), lambda bi, h, qi: (bi, 0, CB_CV + h)),
            pl.BlockSpec((1, tq, LANES), lambda bi, h, qi: (bi, qi, CB_CZ + h)),
            pl.BlockSpec((1, LANES), lambda bi, h, qi: (0, 0)),
        ],
        out_specs=pl.BlockSpec((1, tq, LANES), lambda bi, h, qi: (bi, qi, h)),
        out_shape=jax.ShapeDtypeStruct((b, s, N_HEADS * HEAD_W), BF16),
        compiler_params=_cparams(("parallel", "parallel", "arbitrary")),
        name="diffattn",
    )(scal, qn, kn, proj, proj, subln_g)


def _merge_body(x_ref, ya_ref, yb_ref, yc_ref, g0_ref, g1_ref, g2_ref, wa_ref, wb_ref, wc_ref, wo_ref, o_ref):
    def branch(y_ref, w_ref, g_ref):
        return jax.nn.sigmoid(g_ref[...].astype(F32)) * jnp.dot(y_ref[...], w_ref[...], preferred_element_type=F32)

    merged = branch(ya_ref, wa_ref, g0_ref) + branch(yb_ref, wb_ref, g1_ref) + branch(yc_ref, wc_ref, g2_ref)
    o_ref[...] = x_ref[...] + jnp.dot(merged.astype(BF16), wo_ref[...], preferred_element_type=F32)


def _merge(x2d, ya, yb, yc, proj2d, wa, wb, wc, wo, row0=0, n_rows=None, tm=512):
    n_rows = x2d.shape[0] if n_rows is None else n_rows
    assert row0 % tm == 0 and n_rows % tm == 0
    t0 = row0 // tm
    w_br = N_HEADS * HEAD_W
    row = lambda i: (t0 + i, 0)
    full = lambda i: (0, 0)
    y_spec = pl.BlockSpec((tm, w_br), row)
    wbr_spec = pl.BlockSpec((w_br, D_MODEL), full)
    return pl.pallas_call(
        _merge_body,
        grid=(n_rows // tm,),
        in_specs=[
            pl.BlockSpec((tm, D_MODEL), row),
            y_spec, y_spec, y_spec,
            pl.BlockSpec((tm, D_MODEL), lambda i: (t0 + i, 0)),
            pl.BlockSpec((tm, D_MODEL), lambda i: (t0 + i, 1)),
            pl.BlockSpec((tm, D_MODEL), lambda i: (t0 + i, 2)),
            wbr_spec, wbr_spec, wbr_spec,
            pl.BlockSpec((D_MODEL, D_MODEL), full),
        ],
        out_specs=pl.BlockSpec((tm, D_MODEL), lambda i: (i, 0)),
        out_shape=jax.ShapeDtypeStruct((n_rows, D_MODEL), F32),
        compiler_params=_cparams(("parallel",)),
        name="merge",
    )(x2d, ya, yb, yc, proj2d, proj2d, proj2d, wa, wb, wc, wo)


def _split3(x):
    p1 = x.astype(BF16)
    r1 = x - p1.astype(F32)
    p2 = r1.astype(BF16)
    p3 = (r1 - p2.astype(F32)).astype(BF16)
    return p1, p2, p3


def _softplus(x):
    return jnp.maximum(x, 0.0) + jnp.log1p(jnp.exp(-jnp.abs(x)))


def _silu(x):
    return x * jax.nn.sigmoid(x)


def _dot(a, b):
    return jnp.dot(a, b, preferred_element_type=F32)


def _dot_nt(a, b):
    return lax.dot_general(a, b, (((1,), (1,)), ((), ())), preferred_element_type=F32)


CUM_TILE = 256
GDN_GROUP = 8


def _gdn_body(alog_ref, dtb_ref, q_ref, k_ref, v_ref, z_ref, cwq_ref, cwk_ref, cwv_ref, sm_ref, smt_ref,
              nav_ref, dtv_ref, ng_ref, o_ref,
              xp_ref, qn_ref, kn_ref, vn_ref, gcc_ref, bc_ref, gcr_ref, br_ref,
              lhs_ref, r_ref, gl_ref, of_ref, ob_ref):
    h = pl.program_id(1)
    seq = q_ref.shape[1]
    c = CHUNK_A
    n_chunks = seq // c
    dk = HEAD_W

    row_t = lax.broadcasted_iota(jnp.int32, (CUM_TILE, CUM_TILE), 0)
    col_t = lax.broadcasted_iota(jnp.int32, (CUM_TILE, CUM_TILE), 1)
    same_chunk = (row_t // c) == (col_t // c)
    tri = (jnp.where(same_chunk & (col_t <= row_t), 1.0, 0.0).astype(BF16),
           jnp.where(same_chunk & (col_t >= row_t), 1.0, 0.0).astype(BF16))
    lane_row = lax.broadcasted_iota(jnp.int32, (1, LANES), 1)
    sel_src = lax.broadcasted_iota(jnp.int32, (LANES, 4 * LANES), 0)
    sel_blk = lax.broadcasted_iota(jnp.int32, (LANES, 4 * LANES), 1) // LANES
    sel_want = jnp.where(sel_blk < 2, 2 * N_HEADS + N_HEADS * sel_blk, N_HEADS * (sel_blk - 2)) + h
    sel4 = jnp.where(sel_src == sel_want, 1.0, 0.0).astype(BF16)

    def gates_tile(t, carry):
        rows = pl.ds(pl.multiple_of(t * CUM_TILE, CUM_TILE), CUM_TILE)
        sm = sm_ref[0, rows, :]
        gates = jnp.where(lane_row < 2 * N_HEADS, jax.nn.sigmoid(sm), nav_ref[...] * _softplus(sm + dtv_ref[...]))
        rep = _dot(jnp.concatenate(_split3(gates), axis=0), sel4)
        piece = lambda p, blk: rep[p * CUM_TILE:(p + 1) * CUM_TILE, blk * LANES:(blk + 1) * LANES]
        for d in range(2):
            cs = _dot(tri[d], jnp.concatenate([piece(p, d) for p in range(3)], axis=1).astype(BF16))
            gcc_ref[d, rows, :] = cs[:, :LANES] + cs[:, LANES:2 * LANES] + cs[:, 2 * LANES:]
            bc_ref[d, rows, :] = piece(0, 2 + d) + piece(1, 2 + d) + piece(2, 2 + d)
        return carry

    lax.fori_loop(0, seq // CUM_TILE, gates_tile, 0)

    rc = lax.broadcasted_iota(jnp.int32, (c, c), 0)
    cc = lax.broadcasted_iota(jnp.int32, (c, c), 1)
    for d in range(2):
        a_neg = -jnp.exp(alog_ref[d, h])
        g = a_neg * _softplus(smt_ref[0, 8 + 4 * d + h] + dtb_ref[d, h])
        cum = jnp.where((rc <= cc) if d == 0 else (rc >= cc), 1.0, 0.0).astype(BF16)
        acc = jnp.zeros((n_chunks, c), F32)
        for part in _split3(g):
            acc = acc + _dot(part, cum)
        gcr_ref[d] = acc
        br_ref[d] = jax.nn.sigmoid(smt_ref[0, 4 * d + h])

    pad = 8
    zeros_pad = jnp.zeros((pad, LANES), F32)

    def conv_into(src_ref, cw_ref, dst_ref, normalise, scale):
        xp_ref[pl.ds(0, pad), :] = zeros_pad
        xp_ref[pl.ds(pad + seq, pad), :] = zeros_pad
        xp_ref[pl.ds(pad, seq), :] = src_ref[0].astype(F32)
        cw = cw_ref[...]

        def chunk(n, carry):
            base = pl.multiple_of(n * c, c)
            win = xp_ref[pl.ds(base, c + 2 * pad), :]
            y = jnp.zeros((c, LANES), F32)
            for j in range(CONV_K):
                off = pad - (CONV_K - 1) // 2 + j
                y = y + win[off:off + c, :] * cw[j:j + 1, :]
            y = _silu(y)
            if normalise:
                y = y * lax.rsqrt(jnp.sum(y * y, axis=-1, keepdims=True) + EPS)
            dst_ref[pl.ds(base, c), :] = y * scale if scale != 1.0 else y
            return carry

        lax.fori_loop(0, n_chunks, chunk, 0, unroll=4)

    conv_into(q_ref, cwq_ref, qn_ref, True, dk ** -0.5)
    conv_into(k_ref, cwk_ref, kn_ref, True, 1.0)
    conv_into(v_ref, cwv_ref, vn_ref, False, 1.0)

    eye = jnp.where(rc == cc, 1.0, 0.0)
    masks = (((rc >= cc), (rc > cc)), ((rc <= cc), (rc < cc)))

    def prep_group(i, carry):
        chains = []
        for g in range(GDN_GROUP):
            n = i * GDN_GROUP + g
            rows = pl.ds(pl.multiple_of(n * c, c), c)
            q = qn_ref[rows, :]
            k = kn_ref[rows, :]
            k16 = k.astype(BF16)
            v16 = vn_ref[rows, :].astype(BF16)
            kq = _dot_nt(jnp.concatenate([k16, q.astype(BF16)], axis=0), k16)
            kk, qk = kq[:c, :], kq[c:, :]
            for d in range(2):
                incl, strict = masks[d]
                gcol = gcc_ref[d, rows, :]
                grow = gcr_ref[d, pl.ds(n, 1), :]
                decay = jnp.where(incl, jnp.exp(jnp.where(incl, gcol[:, :c] - grow, 0.0)), 0.0)
                x = -jnp.where(strict, kk * decay * bc_ref[d, rows, :][:, :c], 0.0)
                last = (c - 1) if d == 0 else 0
                g_last = gcol[last:last + 1, :]
                gl_ref[d, pl.ds(n, 1), :] = jnp.exp(g_last)
                chains.append(dict(d=d, n=n, rows=rows, x=x, grow=grow, k16=k16, v16=v16,
                                   a16=(qk * decay).astype(BF16), qd=q * jnp.exp(gcol),
                                   kdt16=(k * jnp.exp(g_last - gcol)).T.astype(BF16)))
        t_inv = [eye + ch["x"] for ch in chains]
        pw16 = [ch["x"].astype(BF16) for ch in chains]
        pw16 = [_dot(p, p).astype(BF16) for p in pw16]
        for _ in range(4):
            both = [_dot(jnp.concatenate([t.astype(BF16), p], axis=0), p) for t, p in zip(t_inv, pw16)]
            t_inv = [t + r[:c, :] for t, r in zip(t_inv, both)]
            pw16 = [r[c:, :].astype(BF16) for r in both]
        t_inv = [t + _dot(t.astype(BF16), p) for t, p in zip(t_inv, pw16)]
        wu = []
        for ch, t in zip(chains, t_inv):
            tb = t * br_ref[ch["d"], pl.ds(ch["n"], 1), :]
            res = _dot(jnp.concatenate([(tb * jnp.exp(ch["grow"])).astype(BF16), tb.astype(BF16)], axis=0),
                       jnp.concatenate([ch["k16"], ch["v16"]], axis=1))
            wu.append(jnp.concatenate([res[:c, :dk], res[c:, dk:]], axis=1).astype(BF16))
        for ch, wu16 in zip(chains, wu):
            d, n = ch["d"], ch["n"]
            res = _dot(jnp.concatenate([ch["kdt16"], ch["a16"]], axis=0), wu16)
            lhs_ref[d, n, pl.ds(0, dk), :] = res[:dk, :dk].astype(BF16)
            lhs_ref[d, n, pl.ds(dk, c), :] = (ch["qd"] - res[dk:, :dk]).astype(BF16)
            r_ref[d, n] = res[:dk, dk:].astype(BF16)
            (of_ref if d == 0 else ob_ref)[ch["rows"], :] = res[dk:, dk:]
        return carry

    lax.fori_loop(0, n_chunks // GDN_GROUP, prep_group, 0)

    def scan_step(i, carry):
        outs = []
        for d, s, o_out in ((0, carry[0], of_ref), (1, carry[1], ob_ref)):
            n = i if d == 0 else n_chunks - 1 - i
            rows = pl.ds(pl.multiple_of(n * c, c), c)
            res = _dot(lhs_ref[d, n], s.astype(BF16))
            o_out[rows, :] = o_out[rows, :] + res[dk:, :]
            outs.append(s * gl_ref[d, pl.ds(n, 1), :] + r_ref[d, n].astype(F32) - res[:dk, :])
        return tuple(outs)

    s0 = jnp.zeros((dk, HEAD_W), F32)
    lax.fori_loop(0, n_chunks, scan_step, (s0, s0))

    def finish(t, carry):
        rows = pl.ds(pl.multiple_of(t * CUM_TILE, CUM_TILE), CUM_TILE)
        o = of_ref[rows, :] + ob_ref[rows, :]
        y = o * lax.rsqrt(jnp.mean(o * o, axis=-1, keepdims=True) + EPS) * ng_ref[...]
        o_ref[0, rows, :] = (y * _silu(z_ref[0, rows, :].astype(F32))).astype(BF16)
        return carry

    lax.fori_loop(0, seq // CUM_TILE, finish, 0)


def _mixer_a(proj, small, small_t, conv_w, a_log, dt_bias, neg_a_vec, dtb_vec, norm_g):
    b, s, _ = proj.shape
    n_chunks = s // CHUNK_A
    assert s % CUM_TILE == 0 and n_chunks % GDN_GROUP == 0
    col = lambda cb: pl.BlockSpec((1, s, LANES), lambda bi, h: (bi, 0, cb + h))
    cw = lambda cb: pl.BlockSpec((CONV_K, LANES), lambda bi, h: (0, cb + h))
    vec = pl.BlockSpec((1, LANES), lambda bi, h: (0, 0))
    smem = pl.BlockSpec(memory_space=pltpu.SMEM)
    return pl.pallas_call(
        _gdn_body,
        grid=(b, N_HEADS),
        in_specs=[
            smem, smem,
            col(CB_AQ), col(CB_AK), col(CB_AV), col(CB_AZ),
            cw(0), cw(4), cw(8),
            pl.BlockSpec((1, s, LANES), lambda bi, h: (bi, 0, 0)),
            pl.BlockSpec((1, N_SMALL, n_chunks, CHUNK_A), lambda bi, h: (bi, 0, 0, 0)),
            vec, vec, vec,
        ],
        out_specs=pl.BlockSpec((1, s, LANES), lambda bi, h: (bi, 0, h)),
        out_shape=jax.ShapeDtypeStruct((b, s, N_HEADS * HEAD_W), BF16),
        scratch_shapes=[
            pltpu.VMEM((s + 16, LANES), F32),
            pltpu.VMEM((s, LANES), F32),
            pltpu.VMEM((s, LANES), F32),
            pltpu.VMEM((s, LANES), F32),
            pltpu.VMEM((2, s, LANES), F32),
            pltpu.VMEM((2, s, LANES), F32),
            pltpu.VMEM((2, n_chunks, CHUNK_A), F32),
            pltpu.VMEM((2, n_chunks, CHUNK_A), F32),
            pltpu.VMEM((2, n_chunks, HEAD_W + CHUNK_A, LANES), BF16),
            pltpu.VMEM((2, n_chunks, HEAD_W, LANES), BF16),
            pltpu.VMEM((2, n_chunks, LANES), F32),
            pltpu.VMEM((s, LANES), F32),
            pltpu.VMEM((s, LANES), F32),
        ],
        compiler_params=_cparams(("parallel", "arbitrary")),
        name="gdn",
    )(a_log, dt_bias, proj, proj, proj, proj, conv_w, conv_w, conv_w, small, small_t, neg_a_vec, dtb_vec, norm_g)


HGRN_GROUP = 4
HGRN_SKEW = 3


def _hgrn_body(q_ref, i_ref, ff_ref, fb_ref, z_ref, lb_ref, ng_ref, o_ref,
               qs_ref, kk_ref, gc_ref, gt_ref, qd_ref, kd_ref, of_ref, ob_ref):
    seq = q_ref.shape[1]
    c = CHUNK_B
    n_chunks = seq // c
    dk = HEAD_W

    row_t = lax.broadcasted_iota(jnp.int32, (CUM_TILE, CUM_TILE), 0)
    col_t = lax.broadcasted_iota(jnp.int32, (CUM_TILE, CUM_TILE), 1)
    same_chunk = (row_t // c) == (col_t // c)
    tri = (jnp.where(same_chunk & (col_t <= row_t), 1.0, 0.0).astype(BF16),
           jnp.where(same_chunk & (col_t >= row_t), 1.0, 0.0).astype(BF16))
    tot = jnp.where(same_chunk, 1.0, 0.0).astype(BF16)

    def gates_tile(t, carry):
        rows = pl.ds(pl.multiple_of(t * CUM_TILE, CUM_TILE), CUM_TILE)
        qs = _silu(q_ref[0, rows, :].astype(F32)) * (dk ** -0.5)
        qs_ref[rows, :] = qs
        for d, f_ref in ((0, ff_ref), (1, fb_ref)):
            bf = f_ref[0, rows, :].astype(F32)
            kk = (1.0 - lb_ref[d:d + 1, :]) * jax.nn.sigmoid(-bf)
            log_f = jnp.log1p(-jnp.minimum(kk, 1.0 - F_MIN_GAP))
            gc = jnp.zeros((CUM_TILE, LANES), F32)
            gt = jnp.zeros((CUM_TILE, LANES), F32)
            for part in _split3(log_f):
                gc = gc + _dot(tri[d], part)
                gt = gt + _dot(tot, part)
            kk_ref[d, rows, :] = kk
            gc_ref[d, rows, :] = gc * LOG2E
            gt_ref[d, rows, :] = gt
            qd_ref[d, rows, :] = (qs * jnp.exp(gc)).astype(BF16)
            kd_ref[d, rows, :] = (kk * jnp.exp(gt - gc)).astype(BF16)
        return carry

    lax.fori_loop(0, seq // CUM_TILE, gates_tile, 0)

    jrow = lax.broadcasted_iota(jnp.int32, (SUB, LANES), 0)
    ones_w = jnp.ones((LANES, LANES), BF16)
    n_pieces = c + c // 2
    piece = lax.broadcasted_iota(jnp.int32, (c, n_pieces * SUB), 1) // SUB
    out_row = lax.broadcasted_iota(jnp.int32, (c, n_pieces * SUB), 0)
    row_of_piece = (jnp.where(piece < SUB, piece, SUB + (piece - SUB) // 2),
                    jnp.where(piece < c, piece // 2, piece - SUB))
    sel = tuple(jnp.where(rp == out_row, 1.0, 0.0).astype(BF16) for rp in row_of_piece)

    def scan_step(i, carry):
        work = []
        for g in range(HGRN_GROUP):
            for d in range(2):
                n = i * HGRN_GROUP + g
                n = n if d == 0 else n_chunks - 1 - n
                work.append((d, pl.ds(pl.multiple_of(n * c, c), c)))
        def build(d, rows):
            qs = qs_ref[rows, :]
            kk = kk_ref[d, rows, :]
            gc2 = gc_ref[d, rows, :]
            v16 = i_ref[0, rows, :]
            v = v16.astype(F32)
            parts, vparts = [], []
            for r in range(c):
                for j0 in (0, SUB):
                    if (j0 > r) if d == 0 else (j0 + SUB - 1 < r):
                        continue
                    pair = jnp.exp2(gc2[r:r + 1, :] - gc2[j0:j0 + SUB, :])
                    if not ((j0 + SUB - 1 <= r) if d == 0 else (j0 >= r)):
                        mask = (jrow + j0 <= r) if d == 0 else (jrow + j0 >= r)
                        pair = jnp.where(mask, pair, 0.0)
                    parts.append(pair * kk[j0:j0 + SUB, :] * qs[r:r + 1, :])
                    vparts.append(v[j0:j0 + SUB, :])
            upd = lax.dot_general(v16, kd_ref[d, rows, :], (((0,), (0,)), ((), ())), preferred_element_type=F32)
            a_rep = _dot(jnp.concatenate(parts, axis=0).astype(BF16), ones_w)
            return a_rep, jnp.concatenate(vparts, axis=0), upd

        st = list(carry)
        built = {}
        for step in range(len(work) + HGRN_SKEW):
            if step < len(work):
                built[step] = build(*work[step])
            w = step - HGRN_SKEW
            if w >= 0:
                d, rows = work[w]
                a_rep, v_stack, upd = built.pop(w)
                o_intra = _dot(sel[d], (a_rep * v_stack).astype(BF16))
                o_inter = _dot_nt(qd_ref[d, rows, :], st[d].astype(BF16))
                (of_ref if d == 0 else ob_ref)[rows, :] = o_intra + o_inter
                e_row = jnp.exp(gt_ref[d, pl.ds(rows.start, 1), :])
                st[d] = st[d] * e_row + upd
        return tuple(st)

    s0 = jnp.zeros((HEAD_W, dk), F32)
    lax.fori_loop(0, n_chunks // HGRN_GROUP, scan_step, (s0, s0))

    def finish(t, carry):
        rows = pl.ds(pl.multiple_of(t * CUM_TILE, CUM_TILE), CUM_TILE)
        o = of_ref[rows, :] + ob_ref[rows, :]
        y = o * lax.rsqrt(jnp.mean(o * o, axis=-1, keepdims=True) + EPS) * ng_ref[...]
        o_ref[0, rows, :] = (y * _silu(z_ref[0, rows, :].astype(F32))).astype(BF16)
        return carry

    lax.fori_loop(0, seq // CUM_TILE, finish, 0)


def _mixer_b(proj, lb, norm_g):
    b, s, _ = proj.shape
    assert s % CUM_TILE == 0 and (s // CHUNK_B) % HGRN_GROUP == 0
    col = lambda cb: pl.BlockSpec((1, s, LANES), lambda bi, h: (bi, 0, cb + h))
    return pl.pallas_call(
        _hgrn_body,
        grid=(b, N_HEADS),
        in_specs=[
            col(CB_BQ), col(CB_BI), col(CB_BF), col(CB_BF + N_HEADS), col(CB_BZ),
            pl.BlockSpec((2, LANES), lambda bi, h: (0, h)),
            pl.BlockSpec((1, LANES), lambda bi, h: (0, 0)),
        ],
        out_specs=pl.BlockSpec((1, s, LANES), lambda bi, h: (bi, 0, h)),
        out_shape=jax.ShapeDtypeStruct((b, s, N_HEADS * HEAD_W), BF16),
        scratch_shapes=[
            pltpu.VMEM((s, LANES), F32),
            pltpu.VMEM((2, s, LANES), F32),
            pltpu.VMEM((2, s, LANES), F32),
            pltpu.VMEM((2, s, LANES), F32),
            pltpu.VMEM((2, s, LANES), BF16),
            pltpu.VMEM((2, s, LANES), BF16),
            pltpu.VMEM((s, LANES), F32),
            pltpu.VMEM((s, LANES), F32),
        ],
        compiler_params=_cparams(("parallel", "arbitrary")),
        name="hgrn",
    )(proj, proj, proj, proj, proj, lb, norm_g)


def _rope_tables(seq):
    half = ROPE_DIM // 2
    inv = 1.0 / (ROPE_THETA ** (jnp.arange(0, ROPE_DIM, 2, dtype=F32) / ROPE_DIM))
    ang = jnp.arange(seq, dtype=F32)[:, None] * inv[None, :]
    cos, sin = jnp.cos(ang), jnp.sin(ang)
    one = jnp.ones((seq, D_C - ROPE_DIM), F32)
    zero = jnp.zeros((seq, D_C - ROPE_DIM), F32)
    zh = jnp.zeros((seq, half), F32)
    c_map = jnp.concatenate([cos, cos, one], axis=1)
    s1_map = jnp.concatenate([zh, sin, zero], axis=1)
    s2_map = jnp.concatenate([-sin, zh, zero], axis=1)
    tile2 = lambda t: jnp.concatenate([t, t], axis=1)
    return tile2(c_map), tile2(s1_map), tile2(s2_map)


def _lower_bounds(lb_logits):
    p = jax.nn.softmax(lb_logits.astype(F32), axis=1)
    return jnp.cumsum(p, axis=1) - p[:, :1]


def kernel(x_prompt, x_sample, norm_g, w_in, conv_w, a_log, dt_bias, gdn_norm_g, hgrn_lb_logits, hgrn_norm_g,
           q_norm_g, k_norm_g, diff_lambda, subln_g, w_br_a, w_br_b, w_br_c, w_out):
    nb_p = x_prompt.shape[0]
    x = jnp.concatenate([x_prompt, x_sample], axis=0)
    b, s, d = x.shape
    depth = w_in.shape[0]
    n_in = w_in.shape[-1]
    gate0 = n_in - 3 * D_MODEL
    w_main = jnp.concatenate([w_in[:, :, gate0:], w_in[:, :, :2048], w_in[:, :, 2048 + N_SMALL:gate0]],
                             axis=-1).astype(BF16)
    w_small = jnp.pad(w_in[:, :, 2048:2048 + N_SMALL], ((0, 0), (0, 0), (0, LANES - N_SMALL))).astype(BF16)
    wa, wb, wc, wo = (w.astype(BF16) for w in (w_br_a, w_br_b, w_br_c, w_out))
    cos_t, s1_t, s2_t = _rope_tables(s)
    lbs = _lower_bounds(hgrn_lb_logits)
    lane_pad = ((0, 0), (2 * N_HEADS, LANES - 4 * N_HEADS))
    neg_a_vec = jnp.pad(-jnp.exp(a_log.astype(F32)).reshape(depth, 2 * N_HEADS), lane_pad)[:, None, :]
    dtb_vec = jnp.pad(dt_bias.astype(F32).reshape(depth, 2 * N_HEADS), lane_pad)[:, None, :]
    lp = diff_lambda.astype(F32)
    lam_dyn = jnp.exp(jnp.sum(lp[:, 0] * lp[:, 1], axis=-1)) - jnp.exp(jnp.sum(lp[:, 2] * lp[:, 3], axis=-1))

    x2d = x.reshape(b * s, d)
    for l in range(depth):
        lambda_init = 0.8 - 0.6 * math.exp(-0.3 * l)
        proj2d, small2d = _inproj(x2d, norm_g[l][None, :], w_main[l], w_small[l])
        proj = proj2d.reshape(b, s, N_MAIN)
        small = small2d.reshape(b, s, LANES)
        small_t = small[:, :, :N_SMALL].transpose(0, 2, 1).reshape(b, N_SMALL, s // CHUNK_A, CHUNK_A)
        ya = _mixer_a(proj, small, small_t, conv_w[l], a_log[l], dt_bias[l], neg_a_vec[l], dtb_vec[l],
                      gdn_norm_g[l][None, :])
        yb = _mixer_b(proj, lbs[:, l], hgrn_norm_g[l][None, :])
        qn, kn = _qkprep(proj, cos_t, s1_t, s2_t, q_norm_g[l].reshape(1, LANES), k_norm_g[l].reshape(1, LANES))
        scal = jnp.stack([lam_dyn[l] + lambda_init, jnp.asarray(1.0 - lambda_init, F32)]).astype(F32)
        yc = _attn(scal, qn, kn, proj, subln_g[l][None, :])
        merge_args = (x2d, ya.reshape(b * s, -1), yb.reshape(b * s, -1), yc.reshape(b * s, -1), proj2d,
                      wa[l], wb[l], wc[l], wo[l])
        if l + 1 < depth:
            x2d = _merge(*merge_args)
    y_prompt = _merge(*merge_args, row0=0, n_rows=nb_p * s)
    y_sample = _merge(*merge_args, row0=nb_p * s, n_rows=(b - nb_p) * s)
    return (y_prompt.reshape(nb_p, s, d), y_sample.reshape(b - nb_p, s, d))
```

```python
import functools
import math

import jax
import jax.numpy as jnp
from jax import lax
from jax.experimental import pallas as pl
from jax.experimental.pallas import tpu as pltpu

F32 = jnp.float32
BF16 = jnp.bfloat16
EPS = 1e-6
F_MIN_GAP = 1e-6
LOG2E = math.log2(math.e)

D_MODEL = 1024
LANES = 128
SUB = 8
N_HEADS = 4
HEAD_W = 128
CONV_K = 5
CHUNK_A = 64
CHUNK_B = 16
D_C = 64
ROPE_DIM = D_C // 4
ROPE_THETA = 500000.0
N_SMALL = 16
N_MAIN = 9728
CB_AQ, CB_AK, CB_AV, CB_AZ = 24, 28, 32, 36
CB_BQ, CB_BI, CB_BF, CB_BZ = 40, 44, 48, 56
CB_CQ, CB_CK, CB_CV, CB_CZ = 60, 64, 68, 72
VMEM_LIMIT = 56 * 1024 * 1024


def _cparams(sem):
    return pltpu.CompilerParams(dimension_semantics=sem, vmem_limit_bytes=VMEM_LIMIT)


def _inproj_body(x_ref, g_ref, w_ref, ws_ref, o_ref, os_ref, h_ref):
    @pl.when(pl.program_id(1) == 0)
    def _():
        x = x_ref[...]
        h = x * lax.rsqrt(jnp.mean(x * x, axis=-1, keepdims=True) + EPS) * g_ref[...]
        hb = h.astype(BF16)
        h_ref[...] = hb
        os_ref[...] = jnp.dot(hb, ws_ref[...], preferred_element_type=F32)

    o_ref[...] = jnp.dot(h_ref[...], w_ref[...], preferred_element_type=F32).astype(BF16)


def _inproj(x2d, g, w_main, w_small, tm=1024, tn=2432):
    t = x2d.shape[0]
    while t % tm:
        tm //= 2
    return pl.pallas_call(
        _inproj_body,
        grid=(t // tm, N_MAIN // tn),
        in_specs=[
            pl.BlockSpec((tm, D_MODEL), lambda i, j: (i, 0)),
            pl.BlockSpec((1, D_MODEL), lambda i, j: (0, 0)),
            pl.BlockSpec((D_MODEL, tn), lambda i, j: (0, j)),
            pl.BlockSpec((D_MODEL, LANES), lambda i, j: (0, 0)),
        ],
        out_specs=[
            pl.BlockSpec((tm, tn), lambda i, j: (i, j)),
            pl.BlockSpec((tm, LANES), lambda i, j: (i, 0)),
        ],
        out_shape=[
            jax.ShapeDtypeStruct((t, N_MAIN), BF16),
            jax.ShapeDtypeStruct((t, LANES), F32),
        ],
        scratch_shapes=[pltpu.VMEM((tm, D_MODEL), BF16)],
        compiler_params=_cparams(("parallel", "arbitrary")),
        name="inproj",
    )(x2d, g, w_main, w_small)


def _qkprep_body(q_ref, k_ref, c_ref, s1_ref, s2_ref, qg_ref, kg_ref, qo_ref, ko_ref, *, q_scale):
    cos, s1, s2 = c_ref[...], s1_ref[...], s2_ref[...]
    same_map = jnp.where(lax.broadcasted_iota(jnp.int32, (LANES, LANES), 0) // D_C
                         == lax.broadcasted_iota(jnp.int32, (LANES, LANES), 1) // D_C, 1.0, 0.0).astype(BF16)

    def prep(x, g):
        x = x.astype(F32)
        x2 = x * x
        x2_hi = x2.astype(BF16)
        x2_lo = (x2 - x2_hi.astype(F32)).astype(BF16)
        ms = (_dot(x2_hi, same_map) + _dot(x2_lo, same_map)) * (1.0 / D_C)
        y = x * lax.rsqrt(ms + EPS) * g
        return y * cos + pltpu.roll(y, ROPE_DIM // 2, 1) * s1 + pltpu.roll(y, LANES - ROPE_DIM // 2, 1) * s2

    qo_ref[0] = (prep(q_ref[0], qg_ref[...]) * q_scale).astype(BF16)
    ko_ref[0] = prep(k_ref[0], kg_ref[...]).astype(BF16)


def _qkprep(proj, cos_t, s1_t, s2_t, qg, kg, ts=1024):
    b, s, _ = proj.shape
    ts = min(ts, s)
    q_scale = (D_C ** -0.5) * math.log2(math.e)
    tab = pl.BlockSpec((ts, LANES), lambda bi, si, h: (si, 0))
    vec = pl.BlockSpec((1, LANES), lambda bi, si, h: (0, 0))
    out = pl.BlockSpec((1, ts, LANES), lambda bi, si, h: (bi, si, h))
    return pl.pallas_call(
        functools.partial(_qkprep_body, q_scale=q_scale),
        grid=(b, s // ts, N_HEADS),
        in_specs=[
            pl.BlockSpec((1, ts, LANES), lambda bi, si, h: (bi, si, CB_CQ + h)),
            pl.BlockSpec((1, ts, LANES), lambda bi, si, h: (bi, si, CB_CK + h)),
            tab, tab, tab, vec, vec,
        ],
        out_specs=[out, out],
        out_shape=[jax.ShapeDtypeStruct((b, s, N_HEADS * HEAD_W), BF16)] * 2,
        compiler_params=_cparams(("parallel", "parallel", "parallel")),
        name="qkprep",
    )(proj, proj, cos_t, s1_t, s2_t, qg, kg)


ATTN_SUB = 256
ATTN_AHEAD = 1


def _attn_body(sc_ref, q_ref, k_ref, v_ref, z_ref, g_ref, o_ref):
    lam = sc_ref[0]
    out_scale = sc_ref[1]
    k = k_ref[0]
    v = v_ref[0]
    lane = lax.broadcasted_iota(jnp.int32, (1, LANES), 1)
    n_sub = q_ref.shape[1] // ATTN_SUB

    def scores(r, m):
        q = q_ref[0, r * ATTN_SUB:(r + 1) * ATTN_SUB, :]
        qm = jnp.where((lane < D_C) if m == 0 else (lane >= D_C), q, jnp.zeros_like(q))
        return _dot_nt(qm, k)

    def weighted(s):
        p = jnp.exp2(s - jnp.max(s, axis=-1, keepdims=True))
        return _dot(p.astype(BF16), v) / jnp.sum(p, axis=-1, keepdims=True)

    units = [(r, m) for r in range(n_sub) for m in range(2)]
    outs = {}
    pending = [scores(*u) for u in units[:ATTN_AHEAD]]
    for idx, unit in enumerate(units):
        if idx + ATTN_AHEAD < len(units):
            pending.append(scores(*units[idx + ATTN_AHEAD]))
        outs[unit] = weighted(pending.pop(0))
    for r in range(n_sub):
        rows = slice(r * ATTN_SUB, (r + 1) * ATTN_SUB)
        o = outs[(r, 0)] - lam * outs[(r, 1)]
        y = o * lax.rsqrt(jnp.mean(o * o, axis=-1, keepdims=True) + EPS) * g_ref[...]
        z = z_ref[0, rows, :].astype(F32)
        o_ref[0, rows, :] = (y * out_scale * (z * jax.nn.sigmoid(z))).astype(BF16)


def _attn(scal, qn, kn, proj, subln_g, tq=1024):
    b, s, _ = proj.shape
    tq = min(tq, s)
    assert tq % ATTN_SUB == 0 and s % tq == 0
    return pl.pallas_call(
        _attn_body,
        grid=(b, N_HEADS, s // tq),
        in_specs=[
            pl.BlockSpec(memory_space=pltpu.SMEM),
            pl.BlockSpec((1, tq, LANES), lambda bi, h, qi: (bi, qi, h)),
            pl.BlockSpec((1, s, LANES), lambda bi, h, qi: (bi, 0, h)),
            pl.BlockSpec((1, s, LANES), lambda bi, h, qi: (bi, 0, CB_CV + h)),
            pl.BlockSpec((1, tq, LANES), lambda bi, h, qi: (bi, qi, CB_CZ + h)),
            pl.BlockSpec((1, LANES), lambda bi, h, qi: (0, 0)),
        ],
        out_specs=pl.BlockSpec((1, tq, LANES), lambda bi, h, qi: (bi, qi, h)),
        out_shape=jax.ShapeDtypeStruct((b, s, N_HEADS * HEAD_W), BF16),
        compiler_params=_cparams(("parallel", "parallel", "arbitrary")),
        name="diffattn",
    )(scal, qn, kn, proj, proj, subln_g)


def _merge_body(x_ref, ya_ref, yb_ref, yc_ref, g0_ref, g1_ref, g2_ref, wa_ref, wb_ref, wc_ref, wo_ref, o_ref):
    def branch(y_ref, w_ref, g_ref):
        return jax.nn.sigmoid(g_ref[...].astype(F32)) * jnp.dot(y_ref[...], w_ref[...], preferred_element_type=F32)

    merged = branch(ya_ref, wa_ref, g0_ref) + branch(yb_ref, wb_ref, g1_ref) + branch(yc_ref, wc_ref, g2_ref)
    o_ref[...] = x_ref[...] + jnp.dot(merged.astype(BF16), wo_ref[...], preferred_element_type=F32)


def _merge(x2d, ya, yb, yc, proj2d, wa, wb, wc, wo, row0=0, n_rows=None, tm=512):
    n_rows = x2d.shape[0] if n_rows is None else n_rows
    assert row0 % tm == 0 and n_rows % tm == 0
    t0 = row0 // tm
    w_br = N_HEADS * HEAD_W
    row = lambda i: (t0 + i, 0)
    full = lambda i: (0, 0)
    y_spec = pl.BlockSpec((tm, w_br), row)
    wbr_spec = pl.BlockSpec((w_br, D_MODEL), full)
    return pl.pallas_call(
        _merge_body,
        grid=(n_rows // tm,),
        in_specs=[
            pl.BlockSpec((tm, D_MODEL), row),
            y_spec, y_spec, y_spec,
            pl.BlockSpec((tm, D_MODEL), lambda i: (t0 + i, 0)),
            pl.BlockSpec((tm, D_MODEL), lambda i: (t0 + i, 1)),
            pl.BlockSpec((tm, D_MODEL), lambda i: (t0 + i, 2)),
            wbr_spec, wbr_spec, wbr_spec,
            pl.BlockSpec((D_MODEL, D_MODEL), full),
        ],
        out_specs=pl.BlockSpec((tm, D_MODEL), lambda i: (i, 0)),
        out_shape=jax.ShapeDtypeStruct((n_rows, D_MODEL), F32),
        compiler_params=_cparams(("parallel",)),
        name="merge",
    )(x2d, ya, yb, yc, proj2d, proj2d, proj2d, wa, wb, wc, wo)


def _split3(x):
    p1 = x.astype(BF16)
    r1 = x - p1.astype(F32)
    p2 = r1.astype(BF16)
    p3 = (r1 - p2.astype(F32)).astype(BF16)
    return p1, p2, p3


def _softplus(x):
    return jnp.maximum(x, 0.0) + jnp.log1p(jnp.exp(-jnp.abs(x)))


def _silu(x):
    return x * jax.nn.sigmoid(x)


def _dot(a, b):
    return jnp.dot(a, b, preferred_element_type=F32)


def _dot_nt(a, b):
    return lax.dot_general(a, b, (((1,), (1,)), ((), ())), preferred_element_type=F32)


CUM_TILE = 256
GDN_GROUP = 8


def _gdn_body(alog_ref, dtb_ref, q_ref, k_ref, v_ref, z_ref, cwq_ref, cwk_ref, cwv_ref, sm_ref, smt_ref,
              nav_ref, dtv_ref, ng_ref, o_ref,
              xp_ref, qn_ref, kn_ref, vn_ref, gcc_ref, bc_ref, gcr_ref, br_ref,
              lhs_ref, r_ref, gl_ref, of_ref, ob_ref):
    h = pl.program_id(1)
    seq = q_ref.shape[1]
    c = CHUNK_A
    n_chunks = seq // c
    dk = HEAD_W

    row_t = lax.broadcasted_iota(jnp.int32, (CUM_TILE, CUM_TILE), 0)
    col_t = lax.broadcasted_iota(jnp.int32, (CUM_TILE, CUM_TILE), 1)
    same_chunk = (row_t // c) == (col_t // c)
    tri = (jnp.where(same_chunk & (col_t <= row_t), 1.0, 0.0).astype(BF16),
           jnp.where(same_chunk & (col_t >= row_t), 1.0, 0.0).astype(BF16))
    lane_row = lax.broadcasted_iota(jnp.int32, (1, LANES), 1)
    sel_src = lax.broadcasted_iota(jnp.int32, (LANES, 4 * LANES), 0)
    sel_blk = lax.broadcasted_iota(jnp.int32, (LANES, 4 * LANES), 1) // LANES
    sel_want = jnp.where(sel_blk < 2, 2 * N_HEADS + N_HEADS * sel_blk, N_HEADS * (sel_blk - 2)) + h
    sel4 = jnp.where(sel_src == sel_want, 1.0, 0.0).astype(BF16)

    def gates_tile(t, carry):
        rows = pl.ds(pl.multiple_of(t * CUM_TILE, CUM_TILE), CUM_TILE)
        sm = sm_ref[0, rows, :]
        gates = jnp.where(lane_row < 2 * N_HEADS, jax.nn.sigmoid(sm), nav_ref[...] * _softplus(sm + dtv_ref[...]))
        rep = _dot(jnp.concatenate(_split3(gates), axis=0), sel4)
        piece = lambda p, blk: rep[p * CUM_TILE:(p + 1) * CUM_TILE, blk * LANES:(blk + 1) * LANES]
        for d in range(2):
            cs = _dot(tri[d], jnp.concatenate([piece(p, d) for p in range(3)], axis=1).astype(BF16))
            gcc_ref[d, rows, :] = cs[:, :LANES] + cs[:, LANES:2 * LANES] + cs[:, 2 * LANES:]
            bc_ref[d, rows, :] = piece(0, 2 + d) + piece(1, 2 + d) + piece(2, 2 + d)
        return carry

    lax.fori_loop(0, seq // CUM_TILE, gates_tile, 0)

    rc = lax.broadcasted_iota(jnp.int32, (c, c), 0)
    cc = lax.broadcasted_iota(jnp.int32, (c, c), 1)
    for d in range(2):
        a_neg = -jnp.exp(alog_ref[d, h])
        g = a_neg * _softplus(smt_ref[0, 8 + 4 * d + h] + dtb_ref[d, h])
        cum = jnp.where((rc <= cc) if d == 0 else (rc >= cc), 1.0, 0.0).astype(BF16)
        acc = jnp.zeros((n_chunks, c), F32)
        for part in _split3(g):
            acc = acc + _dot(part, cum)
        gcr_ref[d] = acc
        br_ref[d] = jax.nn.sigmoid(smt_ref[0, 4 * d + h])

    pad = 8
    zeros_pad = jnp.zeros((pad, LANES), F32)

    def conv_into(src_ref, cw_ref, dst_ref, normalise, scale):
        xp_ref[pl.ds(0, pad), :] = zeros_pad
        xp_ref[pl.ds(pad + seq, pad), :] = zeros_pad
        xp_ref[pl.ds(pad, seq), :] = src_ref[0].astype(F32)
        cw = cw_ref[...]

        def chunk(n, carry):
            base = pl.multiple_of(n * c, c)
            win = xp_ref[pl.ds(base, c + 2 * pad), :]
            y = jnp.zeros((c, LANES), F32)
            for j in range(CONV_K):
                off = pad - (CONV_K - 1) // 2 + j
                y = y + win[off:off + c, :] * cw[j:j + 1, :]
            y = _silu(y)
            if normalise:
                y = y * lax.rsqrt(jnp.sum(y * y, axis=-1, keepdims=True) + EPS)
            dst_ref[pl.ds(base, c), :] = y * scale if scale != 1.0 else y
            return carry

        lax.fori_loop(0, n_chunks, chunk, 0, unroll=4)

    conv_into(q_ref, cwq_ref, qn_ref, True, dk ** -0.5)
    conv_into(k_ref, cwk_ref, kn_ref, True, 1.0)
    conv_into(v_ref, cwv_ref, vn_ref, False, 1.0)

    eye = jnp.where(rc == cc, 1.0, 0.0)
    masks = (((rc >= cc), (rc > cc)), ((rc <= cc), (rc < cc)))

    def prep_group(i, carry):
        chains = []
        for g in range(GDN_GROUP):
            n = i * GDN_GROUP + g
            rows = pl.ds(pl.multiple_of(n * c, c), c)
            q = qn_ref[rows, :]
            k = kn_ref[rows, :]
            k16 = k.astype(BF16)
            v16 = vn_ref[rows, :].astype(BF16)
            kq = _dot_nt(jnp.concatenate([k16, q.astype(BF16)], axis=0), k16)
            kk, qk = kq[:c, :], kq[c:, :]
            for d in range(2):
                incl, strict = masks[d]
                gcol = gcc_ref[d, rows, :]
                grow = gcr_ref[d, pl.ds(n, 1), :]
                decay = jnp.where(incl, jnp.exp(jnp.where(incl, gcol[:, :c] - grow, 0.0)), 0.0)
                x = -jnp.where(strict, kk * decay * bc_ref[d, rows, :][:, :c], 0.0)
                last = (c - 1) if d == 0 else 0
                g_last = gcol[last:last + 1, :]
                gl_ref[d, pl.ds(n, 1), :] = jnp.exp(g_last)
                chains.append(dict(d=d, n=n, rows=rows, x=x, grow=grow, k16=k16, v16=v16,
                                   a16=(qk * decay).astype(BF16), qd=q * jnp.exp(gcol),
                                   kdt16=(k * jnp.exp(g_last - gcol)).T.astype(BF16)))
        t_inv = [eye + ch["x"] for ch in chains]
        pw16 = [ch["x"].astype(BF16) for ch in chains]
        pw16 = [_dot(p, p).astype(BF16) for p in pw16]
        for _ in range(4):
            both = [_dot(jnp.concatenate([t.astype(BF16), p], axis=0), p) for t, p in zip(t_inv, pw16)]
            t_inv = [t + r[:c, :] for t, r in zip(t_inv, both)]
            pw16 = [r[c:, :].astype(BF16) for r in both]
        t_inv = [t + _dot(t.astype(BF16), p) for t, p in zip(t_inv, pw16)]
        wu = []
        for ch, t in zip(chains, t_inv):
            tb = t * br_ref[ch["d"], pl.ds(ch["n"], 1), :]
            res = _dot(jnp.concatenate([(tb * jnp.exp(ch["grow"])).astype(BF16), tb.astype(BF16)], axis=0),
                       jnp.concatenate([ch["k16"], ch["v16"]], axis=1))
            wu.append(jnp.concatenate([res[:c, :dk], res[c:, dk:]], axis=1).astype(BF16))
        for ch, wu16 in zip(chains, wu):
            d, n = ch["d"], ch["n"]
            res = _dot(jnp.concatenate([ch["kdt16"], ch["a16"]], axis=0), wu16)
            lhs_ref[d, n, pl.ds(0, dk), :] = res[:dk, :dk].astype(BF16)
            lhs_ref[d, n, pl.ds(dk, c), :] = (ch["qd"] - res[dk:, :dk]).astype(BF16)
            r_ref[d, n] = res[:dk, dk:].astype(BF16)
            (of_ref if d == 0 else ob_ref)[ch["rows"], :] = res[dk:, dk:]
        return carry

    lax.fori_loop(0, n_chunks // GDN_GROUP, prep_group, 0)

    def scan_step(i, carry):
        outs = []
        for d, s, o_out in ((0, carry[0], of_ref), (1, carry[1], ob_ref)):
            n = i if d == 0 else n_chunks - 1 - i
            rows = pl.ds(pl.multiple_of(n * c, c), c)
            res = _dot(lhs_ref[d, n], s.astype(BF16))
            o_out[rows, :] = o_out[rows, :] + res[dk:, :]
            outs.append(s * gl_ref[d, pl.ds(n, 1), :] + r_ref[d, n].astype(F32) - res[:dk, :])
        return tuple(outs)

    s0 = jnp.zeros((dk, HEAD_W), F32)
    lax.fori_loop(0, n_chunks, scan_step, (s0, s0))

    def finish(t, carry):
        rows = pl.ds(pl.multiple_of(t * CUM_TILE, CUM_TILE), CUM_TILE)
        o = of_ref[rows, :] + ob_ref[rows, :]
        y = o * lax.rsqrt(jnp.mean(o * o, axis=-1, keepdims=True) + EPS) * ng_ref[...]
        o_ref[0, rows, :] = (y * _silu(z_ref[0, rows, :].astype(F32))).astype(BF16)
        return carry

    lax.fori_loop(0, seq // CUM_TILE, finish, 0)


def _mixer_a(proj, small, small_t, conv_w, a_log, dt_bias, neg_a_vec, dtb_vec, norm_g):
    b, s, _ = proj.shape
    n_chunks = s // CHUNK_A
    assert s % CUM_TILE == 0 and n_chunks % GDN_GROUP == 0
    col = lambda cb: pl.BlockSpec((1, s, LANES), lambda bi, h: (bi, 0, cb + h))
    cw = lambda cb: pl.BlockSpec((CONV_K, LANES), lambda bi, h: (0, cb + h))
    vec = pl.BlockSpec((1, LANES), lambda bi, h: (0, 0))
    smem = pl.BlockSpec(memory_space=pltpu.SMEM)
    return pl.pallas_call(
        _gdn_body,
        grid=(b, N_HEADS),
        in_specs=[
            smem, smem,
            col(CB_AQ), col(CB_AK), col(CB_AV), col(CB_AZ),
            cw(0), cw(4), cw(8),
            pl.BlockSpec((1, s, LANES), lambda bi, h: (bi, 0, 0)),
            pl.BlockSpec((1, N_SMALL, n_chunks, CHUNK_A), lambda bi, h: (bi, 0, 0, 0)),
            vec, vec, vec,
        ],
        out_specs=pl.BlockSpec((1, s, LANES), lambda bi, h: (bi, 0, h)),
        out_shape=jax.ShapeDtypeStruct((b, s, N_HEADS * HEAD_W), BF16),
        scratch_shapes=[
            pltpu.VMEM((s + 16, LANES), F32),
            pltpu.VMEM((s, LANES), F32),
            pltpu.VMEM((s, LANES), F32),
            pltpu.VMEM((s, LANES), F32),
            pltpu.VMEM((2, s, LANES), F32),
            pltpu.VMEM((2, s, LANES), F32),
            pltpu.VMEM((2, n_chunks, CHUNK_A), F32),
            pltpu.VMEM((2, n_chunks, CHUNK_A), F32),
            pltpu.VMEM((2, n_chunks, HEAD_W + CHUNK_A, LANES), BF16),
            pltpu.VMEM((2, n_chunks, HEAD_W, LANES), BF16),
            pltpu.VMEM((2, n_chunks, LANES), F32),
            pltpu.VMEM((s, LANES), F32),
            pltpu.VMEM((s, LANES), F32),
        ],
        compiler_params=_cparams(("parallel", "arbitrary")),
        name="gdn",
    )(a_log, dt_bias, proj, proj, proj, proj, conv_w, conv_w, conv_w, small, small_t, neg_a_vec, dtb_vec, norm_g)


HGRN_GROUP = 8


def _hgrn_body(q_ref, i_ref, ff_ref, fb_ref, z_ref, lb_ref, ng_ref, o_ref,
               qs_ref, kk_ref, gc_ref, gt_ref, qd_ref, kd_ref, oi_ref, of_ref, ob_ref):
    seq = q_ref.shape[1]
    c = CHUNK_B
    n_chunks = seq // c
    dk = HEAD_W

    row_t = lax.broadcasted_iota(jnp.int32, (CUM_TILE, CUM_TILE), 0)
    col_t = lax.broadcasted_iota(jnp.int32, (CUM_TILE, CUM_TILE), 1)
    same_chunk = (row_t // c) == (col_t // c)
    tri = (jnp.where(same_chunk & (col_t <= row_t), 1.0, 0.0).astype(BF16),
           jnp.where(same_chunk & (col_t >= row_t), 1.0, 0.0).astype(BF16))
    tot = jnp.where(same_chunk, 1.0, 0.0).astype(BF16)

    def gates_tile(t, carry):
        rows = pl.ds(pl.multiple_of(t * CUM_TILE, CUM_TILE), CUM_TILE)
        qs = _silu(q_ref[0, rows, :].astype(F32)) * (dk ** -0.5)
        qs_ref[rows, :] = qs
        for d, f_ref in ((0, ff_ref), (1, fb_ref)):
            bf = f_ref[0, rows, :].astype(F32)
            kk = (1.0 - lb_ref[d:d + 1, :]) * jax.nn.sigmoid(-bf)
            log_f = jnp.log1p(-jnp.minimum(kk, 1.0 - F_MIN_GAP))
            gc = jnp.zeros((CUM_TILE, LANES), F32)
            gt = jnp.zeros((CUM_TILE, LANES), F32)
            for part in _split3(log_f):
                gc = gc + _dot(tri[d], part)
                gt = gt + _dot(tot, part)
            kk_ref[d, rows, :] = kk
            gc_ref[d, rows, :] = gc * LOG2E
            gt_ref[d, rows, :] = gt
            qd_ref[d, rows, :] = (qs * jnp.exp(gc)).astype(BF16)
            kd_ref[d, rows, :] = (kk * jnp.exp(gt - gc)).astype(BF16)
        return carry

    lax.fori_loop(0, seq // CUM_TILE, gates_tile, 0)

    jrow = lax.broadcasted_iota(jnp.int32, (SUB, LANES), 0)
    ones_w = jnp.ones((LANES, LANES), BF16)
    n_pieces = c + c // 2
    piece = lax.broadcasted_iota(jnp.int32, (c, n_pieces * SUB), 1) // SUB
    out_row = lax.broadcasted_iota(jnp.int32, (c, n_pieces * SUB), 0)
    row_of_piece = (jnp.where(piece < SUB, piece, SUB + (piece - SUB) // 2),
                    jnp.where(piece < c, piece // 2, piece - SUB))
    sel_both = jnp.concatenate([jnp.where(rp == out_row, 1.0, 0.0) for rp in row_of_piece], axis=1).astype(BF16)

    def scan_step(i, carry):
        tiles, vstacks, row_sets = [], [], []
        for g in range(HGRN_GROUP):
            rows = pl.ds(pl.multiple_of((i * HGRN_GROUP + g) * c, c), c)
            qs = qs_ref[rows, :]
            v = i_ref[0, rows, :].astype(F32)
            parts, vparts = [], []
            for d in range(2):
                kk = kk_ref[d, rows, :]
                gc2 = gc_ref[d, rows, :]
                for r in range(c):
                    for j0 in (0, SUB):
                        if (j0 > r) if d == 0 else (j0 + SUB - 1 < r):
                            continue
                        pair = jnp.exp2(gc2[r:r + 1, :] - gc2[j0:j0 + SUB, :])
                        if not ((j0 + SUB - 1 <= r) if d == 0 else (j0 >= r)):
                            mask = (jrow + j0 <= r) if d == 0 else (jrow + j0 >= r)
                            pair = jnp.where(mask, pair, 0.0)
                        parts.append(pair * kk[j0:j0 + SUB, :] * qs[r:r + 1, :])
                        vparts.append(v[j0:j0 + SUB, :])
            tiles.append(jnp.concatenate(parts, axis=0).astype(BF16))
            vstacks.append(jnp.concatenate(vparts, axis=0))
            row_sets.append(rows)
        a_reps = [_dot(t, ones_w) for t in tiles]
        upds = {}
        for g in range(HGRN_GROUP):
            for d in range(2):
                n = i * HGRN_GROUP + g
                n = n if d == 0 else n_chunks - 1 - n
                rows = pl.ds(pl.multiple_of(n * c, c), c)
                upds[(g, d)] = (rows, lax.dot_general(i_ref[0, rows, :], kd_ref[d, rows, :], (((0,), (0,)), ((), ())),
                                                      preferred_element_type=F32))
        for rows, a_rep, v_stack in zip(row_sets, a_reps, vstacks):
            oi_ref[rows, :] = _dot(sel_both, (a_rep * v_stack).astype(BF16))
        st = list(carry)
        for g in range(HGRN_GROUP):
            for d in range(2):
                rows, upd = upds[(g, d)]
                (of_ref if d == 0 else ob_ref)[rows, :] = _dot_nt(qd_ref[d, rows, :], st[d].astype(BF16))
                e_row = jnp.exp(gt_ref[d, pl.ds(rows.start, 1), :])
                st[d] = st[d] * e_row + upd
        return tuple(st)

    s0 = jnp.zeros((HEAD_W, dk), F32)
    lax.fori_loop(0, n_chunks // HGRN_GROUP, scan_step, (s0, s0))

    def finish(t, carry):
        rows = pl.ds(pl.multiple_of(t * CUM_TILE, CUM_TILE), CUM_TILE)
        o = oi_ref[rows, :] + of_ref[rows, :] + ob_ref[rows, :]
        y = o * lax.rsqrt(jnp.mean(o * o, axis=-1, keepdims=True) + EPS) * ng_ref[...]
        o_ref[0, rows, :] = (y * _silu(z_ref[0, rows, :].astype(F32))).astype(BF16)
        return carry

    lax.fori_loop(0, seq // CUM_TILE, finish, 0)


def _mixer_b(proj, lb, norm_g):
    b, s, _ = proj.shape
    assert s % CUM_TILE == 0 and (s // CHUNK_B) % HGRN_GROUP == 0
    col = lambda cb: pl.BlockSpec((1, s, LANES), lambda bi, h: (bi, 0, cb + h))
    return pl.pallas_call(
        _hgrn_body,
        grid=(b, N_HEADS),
        in_specs=[
            col(CB_BQ), col(CB_BI), col(CB_BF), col(CB_BF + N_HEADS), col(CB_BZ),
            pl.BlockSpec((2, LANES), lambda bi, h: (0, h)),
            pl.BlockSpec((1, LANES), lambda bi, h: (0, 0)),
        ],
        out_specs=pl.BlockSpec((1, s, LANES), lambda bi, h: (bi, 0, h)),
        out_shape=jax.ShapeDtypeStruct((b, s, N_HEADS * HEAD_W), BF16),
        scratch_shapes=[
            pltpu.VMEM((s, LANES), F32),
            pltpu.VMEM((2, s, LANES), F32),
            pltpu.VMEM((2, s, LANES), F32),
            pltpu.VMEM((2, s, LANES), F32),
            pltpu.VMEM((2, s, LANES), BF16),
            pltpu.VMEM((2, s, LANES), BF16),
            pltpu.VMEM((s, LANES), F32),
            pltpu.VMEM((s, LANES), F32),
            pltpu.VMEM((s, LANES), F32),
        ],
        compiler_params=_cparams(("parallel", "arbitrary")),
        name="hgrn",
    )(proj, proj, proj, proj, proj, lb, norm_g)


def _rope_tables(seq):
    half = ROPE_DIM // 2
    inv = 1.0 / (ROPE_THETA ** (jnp.arange(0, ROPE_DIM, 2, dtype=F32) / ROPE_DIM))
    ang = jnp.arange(seq, dtype=F32)[:, None] * inv[None, :]
    cos, sin = jnp.cos(ang), jnp.sin(ang)
    one = jnp.ones((seq, D_C - ROPE_DIM), F32)
    zero = jnp.zeros((seq, D_C - ROPE_DIM), F32)
    zh = jnp.zeros((seq, half), F32)
    c_map = jnp.concatenate([cos, cos, one], axis=1)
    s1_map = jnp.concatenate([zh, sin, zero], axis=1)
    s2_map = jnp.concatenate([-sin, zh, zero], axis=1)
    tile2 = lambda t: jnp.concatenate([t, t], axis=1)
    return tile2(c_map), tile2(s1_map), tile2(s2_map)


def _lower_bounds(lb_logits):
    p = jax.nn.softmax(lb_logits.astype(F32), axis=1)
    return jnp.cumsum(p, axis=1) - p[:, :1]


def kernel(x_prompt, x_sample, norm_g, w_in, conv_w, a_log, dt_bias, gdn_norm_g, hgrn_lb_logits, hgrn_norm_g,
           q_norm_g, k_norm_g, diff_lambda, subln_g, w_br_a, w_br_b, w_br_c, w_out):
    nb_p = x_prompt.shape[0]
    x = jnp.concatenate([x_prompt, x_sample], axis=0)
    b, s, d = x.shape
    depth = w_in.shape[0]
    n_in = w_in.shape[-1]
    gate0 = n_in - 3 * D_MODEL
    w_main = jnp.concatenate([w_in[:, :, gate0:], w_in[:, :, :2048], w_in[:, :, 2048 + N_SMALL:gate0]],
                             axis=-1).astype(BF16)
    w_small = jnp.pad(w_in[:, :, 2048:2048 + N_SMALL], ((0, 0), (0, 0), (0, LANES - N_SMALL))).astype(BF16)
    wa, wb, wc, wo = (w.astype(BF16) for w in (w_br_a, w_br_b, w_br_c, w_out))
    cos_t, s1_t, s2_t = _rope_tables(s)
    lbs = _lower_bounds(hgrn_lb_logits)
    lane_pad = ((0, 0), (2 * N_HEADS, LANES - 4 * N_HEADS))
    neg_a_vec = jnp.pad(-jnp.exp(a_log.astype(F32)).reshape(depth, 2 * N_HEADS), lane_pad)[:, None, :]
    dtb_vec = jnp.pad(dt_bias.astype(F32).reshape(depth, 2 * N_HEADS), lane_pad)[:, None, :]
    lp = diff_lambda.astype(F32)
    lam_dyn = jnp.exp(jnp.sum(lp[:, 0] * lp[:, 1], axis=-1)) - jnp.exp(jnp.sum(lp[:, 2] * lp[:, 3], axis=-1))

    x2d = x.reshape(b * s, d)
    for l in range(depth):
        lambda_init = 0.8 - 0.6 * math.exp(-0.3 * l)
        proj2d, small2d = _inproj(x2d, norm_g[l][None, :], w_main[l], w_small[l])
        proj = proj2d.reshape(b, s, N_MAIN)
        small = small2d.reshape(b, s, LANES)
        small_t = small[:, :, :N_SMALL].transpose(0, 2, 1).reshape(b, N_SMALL, s // CHUNK_A, CHUNK_A)
        ya = _mixer_a(proj, small, small_t, conv_w[l], a_log[l], dt_bias[l], neg_a_vec[l], dtb_vec[l],
                      gdn_norm_g[l][None, :])
        yb = _mixer_b(proj, lbs[:, l], hgrn_norm_g[l][None, :])
        qn, kn = _qkprep(proj, cos_t, s1_t, s2_t, q_norm_g[l].reshape(1, LANES), k_norm_g[l].reshape(1, LANES))
        scal = jnp.stack([lam_dyn[l] + lambda_init, jnp.asarray(1.0 - lambda_init, F32)]).astype(F32)
        yc = _attn(scal, qn, kn, proj, subln_g[l][None, :])
        merge_args = (x2d, ya.reshape(b * s, -1), yb.reshape(b * s, -1), yc.reshape(b * s, -1), proj2d,
                      wa[l], wb[l], wc[l], wo[l])
        if l + 1 < depth:
            x2d = _merge(*merge_args)
    y_prompt = _merge(*merge_args, row0=0, n_rows=nb_p * s)
    y_sample = _merge(*merge_args, row0=nb_p * s, n_rows=(b - nb_p) * s)
    return (y_prompt.reshape(nb_p, s, d), y_sample.reshape(b - nb_p, s, d))
```

```python
import functools
import math

import jax
import jax.numpy as jnp
from jax import lax
from jax.experimental import pallas as pl
from jax.experimental.pallas import tpu as pltpu

F32 = jnp.float32
BF16 = jnp.bfloat16
EPS = 1e-6
F_MIN_GAP = 1e-6
LOG2E = math.log2(math.e)

D_MODEL = 1024
LANES = 128
SUB = 8
N_HEADS = 4
HEAD_W = 128
CONV_K = 5
CHUNK_A = 64
CHUNK_B = 16
D_C = 64
ROPE_DIM = D_C // 4
ROPE_THETA = 500000.0
N_SMALL = 16
N_MAIN = 9728
CB_AQ, CB_AK, CB_AV, CB_AZ = 24, 28, 32, 36
CB_BQ, CB_BI, CB_BF, CB_BZ = 40, 44, 48, 56
CB_CQ, CB_CK, CB_CV, CB_CZ = 60, 64, 68, 72
VMEM_LIMIT = 56 * 1024 * 1024


def _cparams(sem):
    return pltpu.CompilerParams(dimension_semantics=sem, vmem_limit_bytes=VMEM_LIMIT)


def _inproj_body(x_ref, g_ref, w_ref, ws_ref, o_ref, os_ref, h_ref):
    @pl.when(pl.program_id(1) == 0)
    def _():
        x = x_ref[...]
        h = x * lax.rsqrt(jnp.mean(x * x, axis=-1, keepdims=True) + EPS) * g_ref[...]
        hb = h.astype(BF16)
        h_ref[...] = hb
        os_ref[...] = jnp.dot(hb, ws_ref[...], preferred_element_type=F32)

    o_ref[...] = jnp.dot(h_ref[...], w_ref[...], preferred_element_type=F32).astype(BF16)


def _inproj(x2d, g, w_main, w_small, tm=1024, tn=2432):
    t = x2d.shape[0]
    while t % tm:
        tm //= 2
    return pl.pallas_call(
        _inproj_body,
        grid=(t // tm, N_MAIN // tn),
        in_specs=[
            pl.BlockSpec((tm, D_MODEL), lambda i, j: (i, 0)),
            pl.BlockSpec((1, D_MODEL), lambda i, j: (0, 0)),
            pl.BlockSpec((D_MODEL, tn), lambda i, j: (0, j)),
            pl.BlockSpec((D_MODEL, LANES), lambda i, j: (0, 0)),
        ],
        out_specs=[
            pl.BlockSpec((tm, tn), lambda i, j: (i, j)),
            pl.BlockSpec((tm, LANES), lambda i, j: (i, 0)),
        ],
        out_shape=[
            jax.ShapeDtypeStruct((t, N_MAIN), BF16),
            jax.ShapeDtypeStruct((t, LANES), F32),
        ],
        scratch_shapes=[pltpu.VMEM((tm, D_MODEL), BF16)],
        compiler_params=_cparams(("parallel", "arbitrary")),
        name="inproj",
    )(x2d, g, w_main, w_small)


def _qkprep_body(q_ref, k_ref, c_ref, s1_ref, s2_ref, qg_ref, kg_ref, qo_ref, ko_ref, *, q_scale):
    cos, s1, s2 = c_ref[...], s1_ref[...], s2_ref[...]
    same_map = jnp.where(lax.broadcasted_iota(jnp.int32, (LANES, LANES), 0) // D_C
                         == lax.broadcasted_iota(jnp.int32, (LANES, LANES), 1) // D_C, 1.0, 0.0).astype(BF16)

    def prep(x, g):
        x = x.astype(F32)
        x2 = x * x
        x2_hi = x2.astype(BF16)
        x2_lo = (x2 - x2_hi.astype(F32)).astype(BF16)
        ms = (_dot(x2_hi, same_map) + _dot(x2_lo, same_map)) * (1.0 / D_C)
        y = x * lax.rsqrt(ms + EPS) * g
        return y * cos + pltpu.roll(y, ROPE_DIM // 2, 1) * s1 + pltpu.roll(y, LANES - ROPE_DIM // 2, 1) * s2

    qo_ref[0] = (prep(q_ref[0], qg_ref[...]) * q_scale).astype(BF16)
    ko_ref[0] = prep(k_ref[0], kg_ref[...]).astype(BF16)


def _qkprep(proj, cos_t, s1_t, s2_t, qg, kg, ts=1024):
    b, s, _ = proj.shape
    ts = min(ts, s)
    q_scale = (D_C ** -0.5) * math.log2(math.e)
    tab = pl.BlockSpec((ts, LANES), lambda bi, si, h: (si, 0))
    vec = pl.BlockSpec((1, LANES), lambda bi, si, h: (0, 0))
    out = pl.BlockSpec((1, ts, LANES), lambda bi, si, h: (bi, si, h))
    return pl.pallas_call(
        functools.partial(_qkprep_body, q_scale=q_scale),
        grid=(b, s // ts, N_HEADS),
        in_specs=[
            pl.BlockSpec((1, ts, LANES), lambda bi, si, h: (bi, si, CB_CQ + h)),
            pl.BlockSpec((1, ts, LANES), lambda bi, si, h: (bi, si, CB_CK + h)),
            tab, tab, tab, vec, vec,
        ],
        out_specs=[out, out],
        out_shape=[jax.ShapeDtypeStruct((b, s, N_HEADS * HEAD_W), BF16)] * 2,
        compiler_params=_cparams(("parallel", "parallel", "parallel")),
        name="qkprep",
    )(proj, proj, cos_t, s1_t, s2_t, qg, kg)


ATTN_SUB = 256
ATTN_AHEAD = 1


def _attn_body(sc_ref, q_ref, k_ref, v_ref, z_ref, g_ref, o_ref):
    lam = sc_ref[0]
    out_scale = sc_ref[1]
    k = k_ref[0]
    v = v_ref[0]
    lane = lax.broadcasted_iota(jnp.int32, (1, LANES), 1)
    n_sub = q_ref.shape[1] // ATTN_SUB

    def scores(r, m):
        q = q_ref[0, r * ATTN_SUB:(r + 1) * ATTN_SUB, :]
        qm = jnp.where((lane < D_C) if m == 0 else (lane >= D_C), q, jnp.zeros_like(q))
        return _dot_nt(qm, k)

    def weighted(s):
        p = jnp.exp2(s - jnp.max(s, axis=-1, keepdims=True))
        return _dot(p.astype(BF16), v) / jnp.sum(p, axis=-1, keepdims=True)

    units = [(r, m) for r in range(n_sub) for m in range(2)]
    outs = {}
    pending = [scores(*u) for u in units[:ATTN_AHEAD]]
    for idx, unit in enumerate(units):
        if idx + ATTN_AHEAD < len(units):
            pending.append(scores(*units[idx + ATTN_AHEAD]))
        outs[unit] = weighted(pending.pop(0))
    for r in range(n_sub):
        rows = slice(r * ATTN_SUB, (r + 1) * ATTN_SUB)
        o = outs[(r, 0)] - lam * outs[(r, 1)]
        y = o * lax.rsqrt(jnp.mean(o * o, axis=-1, keepdims=True) + EPS) * g_ref[...]
        z = z_ref[0, rows, :].astype(F32)
        o_ref[0, rows, :] = (y * out_scale * (z * jax.nn.sigmoid(z))).astype(BF16)


def _attn(scal, qn, kn, proj, subln_g, tq=1024):
    b, s, _ = proj.shape
    tq = min(tq, s)
    assert tq % ATTN_SUB == 0 and s % tq == 0
    return pl.pallas_call(
        _attn_body,
        grid=(b, N_HEADS, s // tq),
        in_specs=[
            pl.BlockSpec(memory_space=pltpu.SMEM),
            pl.BlockSpec((1, tq, LANES), lambda bi, h, qi: (bi, qi, h)),
            pl.BlockSpec((1, s, LANES), lambda bi, h, qi: (bi, 0, h)),
            pl.BlockSpec((1, s, LANES), lambda bi, h, qi: (bi, 0, CB_CV + h)),
            pl.BlockSpec((1, tq, LANES), lambda bi, h, qi: (bi, qi, CB_CZ + h)),
            pl.BlockSpec((1, LANES), lambda bi, h, qi: (0, 0)),
        ],
        out_specs=pl.BlockSpec((1, tq, LANES), lambda bi, h, qi: (bi, qi, h)),
        out_shape=jax.ShapeDtypeStruct((b, s, N_HEADS * HEAD_W), BF16),
        compiler_params=_cparams(("parallel", "parallel", "arbitrary")),
        name="diffattn",
    )(scal, qn, kn, proj, proj, subln_g)


def _merge_body(x_ref, ya_ref, yb_ref, yc_ref, g0_ref, g1_ref, g2_ref, wa_ref, wb_ref, wc_ref, wo_ref, o_ref):
    def branch(y_ref, w_ref, g_ref):
        return jax.nn.sigmoid(g_ref[...].astype(F32)) * jnp.dot(y_ref[...], w_ref[...], preferred_element_type=F32)

    merged = branch(ya_ref, wa_ref, g0_ref) + branch(yb_ref, wb_ref, g1_ref) + branch(yc_ref, wc_ref, g2_ref)
    o_ref[...] = x_ref[...] + jnp.dot(merged.astype(BF16), wo_ref[...], preferred_element_type=F32)


def _merge(x2d, ya, yb, yc, proj2d, wa, wb, wc, wo, row0=0, n_rows=None, tm=512):
    n_rows = x2d.shape[0] if n_rows is None else n_rows
    assert row0 % tm == 0 and n_rows % tm == 0
    t0 = row0 // tm
    w_br = N_HEADS * HEAD_W
    row = lambda i: (t0 + i, 0)
    full = lambda i: (0, 0)
    y_spec = pl.BlockSpec((tm, w_br), row)
    wbr_spec = pl.BlockSpec((w_br, D_MODEL), full)
    return pl.pallas_call(
        _merge_body,
        grid=(n_rows // tm,),
        in_specs=[
            pl.BlockSpec((tm, D_MODEL), row),
            y_spec, y_spec, y_spec,
            pl.BlockSpec((tm, D_MODEL), lambda i: (t0 + i, 0)),
            pl.BlockSpec((tm, D_MODEL), lambda i: (t0 + i, 1)),
            pl.BlockSpec((tm, D_MODEL), lambda i: (t0 + i, 2)),
            wbr_spec, wbr_spec, wbr_spec,
            pl.BlockSpec((D_MODEL, D_MODEL), full),
        ],
        out_specs=pl.BlockSpec((tm, D_MODEL), lambda i: (i, 0)),
        out_shape=jax.ShapeDtypeStruct((n_rows, D_MODEL), F32),
        compiler_params=_cparams(("parallel",)),
        name="merge",
    )(x2d, ya, yb, yc, proj2d, proj2d, proj2d, wa, wb, wc, wo)


def _split3(x):
    p1 = x.astype(BF16)
    r1 = x - p1.astype(F32)
    p2 = r1.astype(BF16)
    p3 = (r1 - p2.astype(F32)).astype(BF16)
    return p1, p2, p3


def _softplus(x):
    return jnp.maximum(x, 0.0) + jnp.log1p(jnp.exp(-jnp.abs(x)))


def _silu(x):
    return x * jax.nn.sigmoid(x)


def _dot(a, b):
    return jnp.dot(a, b, preferred_element_type=F32)


def _dot_nt(a, b):
    return lax.dot_general(a, b, (((1,), (1,)), ((), ())), preferred_element_type=F32)


CUM_TILE = 256
GDN_GROUP = 8


def _gdn_body(alog_ref, dtb_ref, q_ref, k_ref, v_ref, z_ref, cwq_ref, cwk_ref, cwv_ref, sm_ref, smt_ref,
              nav_ref, dtv_ref, ng_ref, o_ref,
              xp_ref, qn_ref, kn_ref, vn_ref, gcc_ref, bc_ref, gcr_ref, br_ref,
              lhs_ref, r_ref, gl_ref, of_ref, ob_ref):
    h = pl.program_id(1)
    seq = q_ref.shape[1]
    c = CHUNK_A
    n_chunks = seq // c
    dk = HEAD_W

    row_t = lax.broadcasted_iota(jnp.int32, (CUM_TILE, CUM_TILE), 0)
    col_t = lax.broadcasted_iota(jnp.int32, (CUM_TILE, CUM_TILE), 1)
    same_chunk = (row_t // c) == (col_t // c)
    tri = (jnp.where(same_chunk & (col_t <= row_t), 1.0, 0.0).astype(BF16),
           jnp.where(same_chunk & (col_t >= row_t), 1.0, 0.0).astype(BF16))
    lane_row = lax.broadcasted_iota(jnp.int32, (1, LANES), 1)
    sel_src = lax.broadcasted_iota(jnp.int32, (LANES, 4 * LANES), 0)
    sel_blk = lax.broadcasted_iota(jnp.int32, (LANES, 4 * LANES), 1) // LANES
    sel_want = jnp.where(sel_blk < 2, 2 * N_HEADS + N_HEADS * sel_blk, N_HEADS * (sel_blk - 2)) + h
    sel4 = jnp.where(sel_src == sel_want, 1.0, 0.0).astype(BF16)

    def gates_tile(t, carry):
        rows = pl.ds(pl.multiple_of(t * CUM_TILE, CUM_TILE), CUM_TILE)
        sm = sm_ref[0, rows, :]
        gates = jnp.where(lane_row < 2 * N_HEADS, jax.nn.sigmoid(sm), nav_ref[...] * _softplus(sm + dtv_ref[...]))
        rep = _dot(jnp.concatenate(_split3(gates), axis=0), sel4)
        piece = lambda p, blk: rep[p * CUM_TILE:(p + 1) * CUM_TILE, blk * LANES:(blk + 1) * LANES]
        for d in range(2):
            cs = _dot(tri[d], jnp.concatenate([piece(p, d) for p in range(3)], axis=1).astype(BF16))
            gcc_ref[d, rows, :] = cs[:, :LANES] + cs[:, LANES:2 * LANES] + cs[:, 2 * LANES:]
            bc_ref[d, rows, :] = piece(0, 2 + d) + piece(1, 2 + d) + piece(2, 2 + d)
        return carry

    lax.fori_loop(0, seq // CUM_TILE, gates_tile, 0, unroll=2)

    rc = lax.broadcasted_iota(jnp.int32, (c, c), 0)
    cc = lax.broadcasted_iota(jnp.int32, (c, c), 1)
    for d in range(2):
        a_neg = -jnp.exp(alog_ref[d, h])
        g = a_neg * _softplus(smt_ref[0, 8 + 4 * d + h] + dtb_ref[d, h])
        cum = jnp.where((rc <= cc) if d == 0 else (rc >= cc), 1.0, 0.0).astype(BF16)
        acc = jnp.zeros((n_chunks, c), F32)
        for part in _split3(g):
            acc = acc + _dot(part, cum)
        gcr_ref[d] = acc
        br_ref[d] = jax.nn.sigmoid(smt_ref[0, 4 * d + h])

    pad = 8
    zeros_pad = jnp.zeros((pad, LANES), F32)

    def conv_into(src_ref, cw_ref, dst_ref, normalise, scale):
        xp_ref[pl.ds(0, pad), :] = zeros_pad
        xp_ref[pl.ds(pad + seq, pad), :] = zeros_pad
        xp_ref[pl.ds(pad, seq), :] = src_ref[0].astype(F32)
        cw = cw_ref[...]

        def chunk(n, carry):
            base = pl.multiple_of(n * c, c)
            y = jnp.zeros((c, LANES), F32)
            for j in range(CONV_K):
                off = pad - (CONV_K - 1) // 2 + j
                y = y + xp_ref[pl.ds(base + off, c), :] * cw[j:j + 1, :]
            y = _silu(y)
            if normalise:
                y = y * lax.rsqrt(jnp.sum(y * y, axis=-1, keepdims=True) + EPS)
            dst_ref[pl.ds(base, c), :] = y * scale if scale != 1.0 else y
            return carry

        lax.fori_loop(0, n_chunks, chunk, 0, unroll=8)

    conv_into(q_ref, cwq_ref, qn_ref, True, dk ** -0.5)
    conv_into(k_ref, cwk_ref, kn_ref, True, 1.0)
    conv_into(v_ref, cwv_ref, vn_ref, False, 1.0)

    eye = jnp.where(rc == cc, 1.0, 0.0)
    masks = (((rc >= cc), (rc > cc)), ((rc <= cc), (rc < cc)))

    def prep_group(i, carry):
        chains = []
        for g in range(GDN_GROUP):
            n = i * GDN_GROUP + g
            rows = pl.ds(pl.multiple_of(n * c, c), c)
            q = qn_ref[rows, :]
            k = kn_ref[rows, :]
            k16 = k.astype(BF16)
            v16 = vn_ref[rows, :].astype(BF16)
            kq = _dot_nt(jnp.concatenate([k16, q.astype(BF16)], axis=0), k16)
            kk, qk = kq[:c, :], kq[c:, :]
            for d in range(2):
                incl, strict = masks[d]
                gcol = gcc_ref[d, rows, :]
                grow = gcr_ref[d, pl.ds(n, 1), :]
                decay = jnp.where(incl, jnp.exp(jnp.where(incl, gcol[:, :c] - grow, 0.0)), 0.0)
                x = -jnp.where(strict, kk * decay * bc_ref[d, rows, :][:, :c], 0.0)
                last = (c - 1) if d == 0 else 0
                g_last = gcol[last:last + 1, :]
                gl_ref[d, pl.ds(n, 1), :] = jnp.exp(g_last)
                chains.append(dict(d=d, n=n, rows=rows, x=x, grow=grow, k16=k16, v16=v16,
                                   a16=(qk * decay).astype(BF16), qd=q * jnp.exp(gcol),
                                   kdt16=(k * jnp.exp(g_last - gcol)).T.astype(BF16)))
        t_inv = [eye + ch["x"] for ch in chains]
        pw16 = [ch["x"].astype(BF16) for ch in chains]
        pw16 = [_dot(p, p).astype(BF16) for p in pw16]
        for _ in range(4):
            both = [_dot(jnp.concatenate([t.astype(BF16), p], axis=0), p) for t, p in zip(t_inv, pw16)]
            t_inv = [t + r[:c, :] for t, r in zip(t_inv, both)]
            pw16 = [r[c:, :].astype(BF16) for r in both]
        t_inv = [t + _dot(t.astype(BF16), p) for t, p in zip(t_inv, pw16)]
        wu = []
        for ch, t in zip(chains, t_inv):
            tb = t * br_ref[ch["d"], pl.ds(ch["n"], 1), :]
            res = _dot(jnp.concatenate([(tb * jnp.exp(ch["grow"])).astype(BF16), tb.astype(BF16)], axis=0),
                       jnp.concatenate([ch["k16"], ch["v16"]], axis=1))
            wu.append(jnp.concatenate([res[:c, :dk], res[c:, dk:]], axis=1).astype(BF16))
        for ch, wu16 in zip(chains, wu):
            d, n = ch["d"], ch["n"]
            res = _dot(jnp.concatenate([ch["kdt16"], ch["a16"]], axis=0), wu16)
            lhs_ref[d, n, pl.ds(0, dk), :] = res[:dk, :dk].astype(BF16)
            lhs_ref[d, n, pl.ds(dk, c), :] = (ch["qd"] - res[dk:, :dk]).astype(BF16)
            r_ref[d, n] = res[:dk, dk:].astype(BF16)
            (of_ref if d == 0 else ob_ref)[ch["rows"], :] = res[dk:, dk:]
        return carry

    lax.fori_loop(0, n_chunks // GDN_GROUP, prep_group, 0)

    def scan_step(i, carry):
        outs = []
        for d, s, o_out in ((0, carry[0], of_ref), (1, carry[1], ob_ref)):
            n = i if d == 0 else n_chunks - 1 - i
            rows = pl.ds(pl.multiple_of(n * c, c), c)
            res = _dot(lhs_ref[d, n], s.astype(BF16))
            o_out[rows, :] = o_out[rows, :] + res[dk:, :]
            outs.append(s * gl_ref[d, pl.ds(n, 1), :] + r_ref[d, n].astype(F32) - res[:dk, :])
        return tuple(outs)

    s0 = jnp.zeros((dk, HEAD_W), F32)
    lax.fori_loop(0, n_chunks, scan_step, (s0, s0))

    def finish(t, carry):
        rows = pl.ds(pl.multiple_of(t * CUM_TILE, CUM_TILE), CUM_TILE)
        o = of_ref[rows, :] + ob_ref[rows, :]
        y = o * lax.rsqrt(jnp.mean(o * o, axis=-1, keepdims=True) + EPS) * ng_ref[...]
        o_ref[0, rows, :] = (y * _silu(z_ref[0, rows, :].astype(F32))).astype(BF16)
        return carry

    lax.fori_loop(0, seq // CUM_TILE, finish, 0, unroll=2)


def _mixer_a(proj, small, small_t, conv_w, a_log, dt_bias, neg_a_vec, dtb_vec, norm_g):
    b, s, _ = proj.shape
    n_chunks = s // CHUNK_A
    assert s % CUM_TILE == 0 and n_chunks % GDN_GROUP == 0
    col = lambda cb: pl.BlockSpec((1, s, LANES), lambda bi, h: (bi, 0, cb + h))
    cw = lambda cb: pl.BlockSpec((CONV_K, LANES), lambda bi, h: (0, cb + h))
    vec = pl.BlockSpec((1, LANES), lambda bi, h: (0, 0))
    smem = pl.BlockSpec(memory_space=pltpu.SMEM)
    return pl.pallas_call(
        _gdn_body,
        grid=(b, N_HEADS),
        in_specs=[
            smem, smem,
            col(CB_AQ), col(CB_AK), col(CB_AV), col(CB_AZ),
            cw(0), cw(4), cw(8),
            pl.BlockSpec((1, s, LANES), lambda bi, h: (bi, 0, 0)),
            pl.BlockSpec((1, N_SMALL, n_chunks, CHUNK_A), lambda bi, h: (bi, 0, 0, 0)),
            vec, vec, vec,
        ],
        out_specs=pl.BlockSpec((1, s, LANES), lambda bi, h: (bi, 0, h)),
        out_shape=jax.ShapeDtypeStruct((b, s, N_HEADS * HEAD_W), BF16),
        scratch_shapes=[
            pltpu.VMEM((s + 16, LANES), F32),
            pltpu.VMEM((s, LANES), F32),
            pltpu.VMEM((s, LANES), F32),
            pltpu.VMEM((s, LANES), F32),
            pltpu.VMEM((2, s, LANES), F32),
            pltpu.VMEM((2, s, LANES), F32),
            pltpu.VMEM((2, n_chunks, CHUNK_A), F32),
            pltpu.VMEM((2, n_chunks, CHUNK_A), F32),
            pltpu.VMEM((2, n_chunks, HEAD_W + CHUNK_A, LANES), BF16),
            pltpu.VMEM((2, n_chunks, HEAD_W, LANES), BF16),
            pltpu.VMEM((2, n_chunks, LANES), F32),
            pltpu.VMEM((s, LANES), F32),
            pltpu.VMEM((s, LANES), F32),
        ],
        compiler_params=_cparams(("parallel", "arbitrary")),
        name="gdn",
    )(a_log, dt_bias, proj, proj, proj, proj, conv_w, conv_w, conv_w, small, small_t, neg_a_vec, dtb_vec, norm_g)


HGRN_GROUP = 8


def _hgrn_body(q_ref, i_ref, ff_ref, fb_ref, z_ref, lb_ref, ng_ref, o_ref,
               qs_ref, kk_ref, gc_ref, gt_ref, qd_ref, kd_ref, oi_ref, of_ref, ob_ref):
    seq = q_ref.shape[1]
    c = CHUNK_B
    n_chunks = seq // c
    dk = HEAD_W

    row_t = lax.broadcasted_iota(jnp.int32, (CUM_TILE, CUM_TILE), 0)
    col_t = lax.broadcasted_iota(jnp.int32, (CUM_TILE, CUM_TILE), 1)
    same_chunk = (row_t // c) == (col_t // c)
    tri = (jnp.where(same_chunk & (col_t <= row_t), 1.0, 0.0).astype(BF16),
           jnp.where(same_chunk & (col_t >= row_t), 1.0, 0.0).astype(BF16))

    def gates_tile(t, carry):
        rows = pl.ds(pl.multiple_of(t * CUM_TILE, CUM_TILE), CUM_TILE)
        qs = _silu(q_ref[0, rows, :].astype(F32)) * (dk ** -0.5)
        qs_ref[rows, :] = qs
        for d, f_ref in ((0, ff_ref), (1, fb_ref)):
            bf = f_ref[0, rows, :].astype(F32)
            kk = (1.0 - lb_ref[d:d + 1, :]) * jax.nn.sigmoid(-bf)
            log_f = jnp.log1p(-jnp.minimum(kk, 1.0 - F_MIN_GAP))
            gc = jnp.zeros((CUM_TILE, LANES), F32)
            for part in _split3(log_f):
                gc = gc + _dot(tri[d], part)
            end = (c - 1) if d == 0 else 0
            gt = jnp.concatenate([jnp.broadcast_to(gc[k * c + end:k * c + end + 1, :], (c, LANES))
                                  for k in range(CUM_TILE // c)], axis=0)
            kk_ref[d, rows, :] = kk
            gc_ref[d, rows, :] = gc * LOG2E
            gt_ref[d, rows, :] = gt
            qd_ref[d, rows, :] = (qs * jnp.exp(gc)).astype(BF16)
            kd_ref[d, rows, :] = (kk * jnp.exp(gt - gc)).astype(BF16)
        return carry

    lax.fori_loop(0, seq // CUM_TILE, gates_tile, 0, unroll=2)

    jrow = lax.broadcasted_iota(jnp.int32, (SUB, LANES), 0)
    ones_w = jnp.ones((LANES, LANES), BF16)
    n_pieces = c + c // 2
    piece = lax.broadcasted_iota(jnp.int32, (c, n_pieces * SUB), 1) // SUB
    out_row = lax.broadcasted_iota(jnp.int32, (c, n_pieces * SUB), 0)
    row_of_piece = (jnp.where(piece < SUB, piece, SUB + (piece - SUB) // 2),
                    jnp.where(piece < c, piece // 2, piece - SUB))
    sel_both = jnp.concatenate([jnp.where(rp == out_row, 1.0, 0.0) for rp in row_of_piece], axis=1).astype(BF16)

    def scan_step(i, carry):
        tiles, vstacks, row_sets = [], [], []
        for g in range(HGRN_GROUP):
            rows = pl.ds(pl.multiple_of((i * HGRN_GROUP + g) * c, c), c)
            qs = qs_ref[rows, :]
            v = i_ref[0, rows, :].astype(F32)
            parts, vparts = [], []
            for d in range(2):
                kk = kk_ref[d, rows, :]
                gc2 = gc_ref[d, rows, :]
                for r in range(c):
                    for j0 in (0, SUB):
                        if (j0 > r) if d == 0 else (j0 + SUB - 1 < r):
                            continue
                        pair = jnp.exp2(gc2[r:r + 1, :] - gc2[j0:j0 + SUB, :])
                        if not ((j0 + SUB - 1 <= r) if d == 0 else (j0 >= r)):
                            mask = (jrow + j0 <= r) if d == 0 else (jrow + j0 >= r)
                            pair = jnp.where(mask, pair, 0.0)
                        parts.append(pair * kk[j0:j0 + SUB, :] * qs[r:r + 1, :])
                        vparts.append(v[j0:j0 + SUB, :])
            tiles.append(jnp.concatenate(parts, axis=0).astype(BF16))
            vstacks.append(jnp.concatenate(vparts, axis=0))
            row_sets.append(rows)
        a_reps = [_dot(t, ones_w) for t in tiles]
        upds = {}
        for g in range(HGRN_GROUP):
            for d in range(2):
                n = i * HGRN_GROUP + g
                n = n if d == 0 else n_chunks - 1 - n
                rows = pl.ds(pl.multiple_of(n * c, c), c)
                upds[(g, d)] = (rows, lax.dot_general(i_ref[0, rows, :], kd_ref[d, rows, :], (((0,), (0,)), ((), ())),
                                                      preferred_element_type=F32))
        for rows, a_rep, v_stack in zip(row_sets, a_reps, vstacks):
            oi_ref[rows, :] = _dot(sel_both, (a_rep * v_stack).astype(BF16))
        st = list(carry)
        for g in range(HGRN_GROUP):
            for d in range(2):
                rows, upd = upds[(g, d)]
                (of_ref if d == 0 else ob_ref)[rows, :] = _dot_nt(qd_ref[d, rows, :], st[d].astype(BF16))
                e_row = jnp.exp(gt_ref[d, pl.ds(rows.start, 1), :])
                st[d] = st[d] * e_row + upd
        return tuple(st)

    s0 = jnp.zeros((HEAD_W, dk), F32)
    lax.fori_loop(0, n_chunks // HGRN_GROUP, scan_step, (s0, s0))

    def finish(t, carry):
        rows = pl.ds(pl.multiple_of(t * CUM_TILE, CUM_TILE), CUM_TILE)
        o = oi_ref[rows, :] + of_ref[rows, :] + ob_ref[rows, :]
        y = o * lax.rsqrt(jnp.mean(o * o, axis=-1, keepdims=True) + EPS) * ng_ref[...]
        o_ref[0, rows, :] = (y * _silu(z_ref[0, rows, :].astype(F32))).astype(BF16)
        return carry

    lax.fori_loop(0, seq // CUM_TILE, finish, 0, unroll=2)


def _mixer_b(proj, lb, norm_g):
    b, s, _ = proj.shape
    assert s % CUM_TILE == 0 and (s // CHUNK_B) % HGRN_GROUP == 0
    col = lambda cb: pl.BlockSpec((1, s, LANES), lambda bi, h: (bi, 0, cb + h))
    return pl.pallas_call(
        _hgrn_body,
        grid=(b, N_HEADS),
        in_specs=[
            col(CB_BQ), col(CB_BI), col(CB_BF), col(CB_BF + N_HEADS), col(CB_BZ),
            pl.BlockSpec((2, LANES), lambda bi, h: (0, h)),
            pl.BlockSpec((1, LANES), lambda bi, h: (0, 0)),
        ],
        out_specs=pl.BlockSpec((1, s, LANES), lambda bi, h: (bi, 0, h)),
        out_shape=jax.ShapeDtypeStruct((b, s, N_HEADS * HEAD_W), BF16),
        scratch_shapes=[
            pltpu.VMEM((s, LANES), F32),
            pltpu.VMEM((2, s, LANES), F32),
            pltpu.VMEM((2, s, LANES), F32),
            pltpu.VMEM((2, s, LANES), F32),
            pltpu.VMEM((2, s, LANES), BF16),
            pltpu.VMEM((2, s, LANES), BF16),
            pltpu.VMEM((s, LANES), F32),
            pltpu.VMEM((s, LANES), F32),
            pltpu.VMEM((s, LANES), F32),
        ],
        compiler_params=_cparams(("parallel", "arbitrary")),
        name="hgrn",
    )(proj, proj, proj, proj, proj, lb, norm_g)


def _rope_tables(seq):
    half = ROPE_DIM // 2
    inv = 1.0 / (ROPE_THETA ** (jnp.arange(0, ROPE_DIM, 2, dtype=F32) / ROPE_DIM))
    ang = jnp.arange(seq, dtype=F32)[:, None] * inv[None, :]
    cos, sin = jnp.cos(ang), jnp.sin(ang)
    one = jnp.ones((seq, D_C - ROPE_DIM), F32)
    zero = jnp.zeros((seq, D_C - ROPE_DIM), F32)
    zh = jnp.zeros((seq, half), F32)
    c_map = jnp.concatenate([cos, cos, one], axis=1)
    s1_map = jnp.concatenate([zh, sin, zero], axis=1)
    s2_map = jnp.concatenate([-sin, zh, zero], axis=1)
    tile2 = lambda t: jnp.concatenate([t, t], axis=1)
    return tile2(c_map), tile2(s1_map), tile2(s2_map)


def _lower_bounds(lb_logits):
    p = jax.nn.softmax(lb_logits.astype(F32), axis=1)
    return jnp.cumsum(p, axis=1) - p[:, :1]


def kernel(x_prompt, x_sample, norm_g, w_in, conv_w, a_log, dt_bias, gdn_norm_g, hgrn_lb_logits, hgrn_norm_g,
           q_norm_g, k_norm_g, diff_lambda, subln_g, w_br_a, w_br_b, w_br_c, w_out):
    nb_p = x_prompt.shape[0]
    x = jnp.concatenate([x_prompt, x_sample], axis=0)
    b, s, d = x.shape
    depth = w_in.shape[0]
    n_in = w_in.shape[-1]
    gate0 = n_in - 3 * D_MODEL
    w_main = jnp.concatenate([w_in[:, :, gate0:], w_in[:, :, :2048], w_in[:, :, 2048 + N_SMALL:gate0]],
                             axis=-1).astype(BF16)
    w_small = jnp.pad(w_in[:, :, 2048:2048 + N_SMALL], ((0, 0), (0, 0), (0, LANES - N_SMALL))).astype(BF16)
    wa, wb, wc, wo = (w.astype(BF16) for w in (w_br_a, w_br_b, w_br_c, w_out))
    cos_t, s1_t, s2_t = _rope_tables(s)
    lbs = _lower_bounds(hgrn_lb_logits)
    lane_pad = ((0, 0), (2 * N_HEADS, LANES - 4 * N_HEADS))
    neg_a_vec = jnp.pad(-jnp.exp(a_log.astype(F32)).reshape(depth, 2 * N_HEADS), lane_pad)[:, None, :]
    dtb_vec = jnp.pad(dt_bias.astype(F32).reshape(depth, 2 * N_HEADS), lane_pad)[:, None, :]
    lp = diff_lambda.astype(F32)
    lam_dyn = jnp.exp(jnp.sum(lp[:, 0] * lp[:, 1], axis=-1)) - jnp.exp(jnp.sum(lp[:, 2] * lp[:, 3], axis=-1))

    x2d = x.reshape(b * s, d)
    for l in range(depth):
        lambda_init = 0.8 - 0.6 * math.exp(-0.3 * l)
        proj2d, small2d = _inproj(x2d, norm_g[l][None, :], w_main[l], w_small[l])
        proj = proj2d.reshape(b, s, N_MAIN)
        small = small2d.reshape(b, s, LANES)
        small_t = small[:, :, :N_SMALL].transpose(0, 2, 1).reshape(b, N_SMALL, s // CHUNK_A, CHUNK_A)
        ya = _mixer_a(proj, small, small_t, conv_w[l], a_log[l], dt_bias[l], neg_a_vec[l], dtb_vec[l],
                      gdn_norm_g[l][None, :])
        yb = _mixer_b(proj, lbs[:, l], hgrn_norm_g[l][None, :])
        qn, kn = _qkprep(proj, cos_t, s1_t, s2_t, q_norm_g[l].reshape(1, LANES), k_norm_g[l].reshape(1, LANES))
        scal = jnp.stack([lam_dyn[l] + lambda_init, jnp.asarray(1.0 - lambda_init, F32)]).astype(F32)
        yc = _attn(scal, qn, kn, proj, subln_g[l][None, :])
        merge_args = (x2d, ya.reshape(b * s, -1), yb.reshape(b * s, -1), yc.reshape(b * s, -1), proj2d,
                      wa[l], wb[l], wc[l], wo[l])
        if l + 1 < depth:
            x2d = _merge(*merge_args)
    y_prompt = _merge(*merge_args, row0=0, n_rows=nb_p * s)
    y_sample = _merge(*merge_args, row0=nb_p * s, n_rows=(b - nb_p) * s)
    return (y_prompt.reshape(nb_p, s, d), y_sample.reshape(b - nb_p, s, d))
```

```python
import functools
import math

import jax
import jax.numpy as jnp
from jax import lax
from jax.experimental import pallas as pl
from jax.experimental.pallas import tpu as pltpu

F32 = jnp.float32
BF16 = jnp.bfloat16
EPS = 1e-6
F_MIN_GAP = 1e-6
LOG2E = math.log2(math.e)

D_MODEL = 1024
LANES = 128
SUB = 8
N_HEADS = 4
HEAD_W = 128
CONV_K = 5
CHUNK_A = 64
CHUNK_B = 16
D_C = 64
ROPE_DIM = D_C // 4
ROPE_THETA = 500000.0
N_SMALL = 16
N_MAIN = 9728
CB_AQ, CB_AK, CB_AV, CB_AZ = 24, 28, 32, 36
CB_BQ, CB_BI, CB_BF, CB_BZ = 40, 44, 48, 56
CB_CQ, CB_CK, CB_CV, CB_CZ = 60, 64, 68, 72
VMEM_LIMIT = 56 * 1024 * 1024


def _cparams(sem):
    return pltpu.CompilerParams(dimension_semantics=sem, vmem_limit_bytes=VMEM_LIMIT)


def _inproj_body(x_ref, g_ref, w_ref, ws_ref, o_ref, os_ref, h_ref):
    @pl.when(pl.program_id(1) == 0)
    def _():
        x = x_ref[...]
        h = x * lax.rsqrt(jnp.mean(x * x, axis=-1, keepdims=True) + EPS) * g_ref[...]
        hb = h.astype(BF16)
        h_ref[...] = hb
        os_ref[...] = jnp.dot(hb, ws_ref[...], preferred_element_type=F32)

    o_ref[...] = jnp.dot(h_ref[...], w_ref[...], preferred_element_type=F32).astype(BF16)


def _inproj(x2d, g, w_main, w_small, tm=1024, tn=2432):
    t = x2d.shape[0]
    while t % tm:
        tm //= 2
    return pl.pallas_call(
        _inproj_body,
        grid=(t // tm, N_MAIN // tn),
        in_specs=[
            pl.BlockSpec((tm, D_MODEL), lambda i, j: (i, 0)),
            pl.BlockSpec((1, D_MODEL), lambda i, j: (0, 0)),
            pl.BlockSpec((D_MODEL, tn), lambda i, j: (0, j)),
            pl.BlockSpec((D_MODEL, LANES), lambda i, j: (0, 0)),
        ],
        out_specs=[
            pl.BlockSpec((tm, tn), lambda i, j: (i, j)),
            pl.BlockSpec((tm, LANES), lambda i, j: (i, 0)),
        ],
        out_shape=[
            jax.ShapeDtypeStruct((t, N_MAIN), BF16),
            jax.ShapeDtypeStruct((t, LANES), F32),
        ],
        scratch_shapes=[pltpu.VMEM((tm, D_MODEL), BF16)],
        compiler_params=_cparams(("parallel", "arbitrary")),
        name="inproj",
    )(x2d, g, w_main, w_small)


def _qkprep_body(q_ref, k_ref, c_ref, s1_ref, s2_ref, qg_ref, kg_ref, qo_ref, ko_ref, *, q_scale):
    cos, s1, s2 = c_ref[...], s1_ref[...], s2_ref[...]
    same_map = jnp.where(lax.broadcasted_iota(jnp.int32, (LANES, LANES), 0) // D_C
                         == lax.broadcasted_iota(jnp.int32, (LANES, LANES), 1) // D_C, 1.0, 0.0).astype(BF16)

    def prep(x, g):
        x = x.astype(F32)
        x2 = x * x
        x2_hi = x2.astype(BF16)
        x2_lo = (x2 - x2_hi.astype(F32)).astype(BF16)
        ms = (_dot(x2_hi, same_map) + _dot(x2_lo, same_map)) * (1.0 / D_C)
        y = x * lax.rsqrt(ms + EPS) * g
        return y * cos + pltpu.roll(y, ROPE_DIM // 2, 1) * s1 + pltpu.roll(y, LANES - ROPE_DIM // 2, 1) * s2

    qo_ref[0] = (prep(q_ref[0], qg_ref[...]) * q_scale).astype(BF16)
    ko_ref[0] = prep(k_ref[0], kg_ref[...]).astype(BF16)


def _qkprep(proj, cos_t, s1_t, s2_t, qg, kg, ts=1024):
    b, s, _ = proj.shape
    ts = min(ts, s)
    q_scale = (D_C ** -0.5) * math.log2(math.e)
    tab = pl.BlockSpec((ts, LANES), lambda bi, si, h: (si, 0))
    vec = pl.BlockSpec((1, LANES), lambda bi, si, h: (0, 0))
    out = pl.BlockSpec((1, ts, LANES), lambda bi, si, h: (bi, si, h))
    return pl.pallas_call(
        functools.partial(_qkprep_body, q_scale=q_scale),
        grid=(b, s // ts, N_HEADS),
        in_specs=[
            pl.BlockSpec((1, ts, LANES), lambda bi, si, h: (bi, si, CB_CQ + h)),
            pl.BlockSpec((1, ts, LANES), lambda bi, si, h: (bi, si, CB_CK + h)),
            tab, tab, tab, vec, vec,
        ],
        out_specs=[out, out],
        out_shape=[jax.ShapeDtypeStruct((b, s, N_HEADS * HEAD_W), BF16)] * 2,
        compiler_params=_cparams(("parallel", "parallel", "parallel")),
        name="qkprep",
    )(proj, proj, cos_t, s1_t, s2_t, qg, kg)


ATTN_SUB = 256
ATTN_AHEAD = 1


def _attn_body(sc_ref, q_ref, k_ref, v_ref, z_ref, g_ref, o_ref):
    lam = sc_ref[0]
    out_scale = sc_ref[1]
    k = k_ref[0]
    v = v_ref[0]
    lane = lax.broadcasted_iota(jnp.int32, (1, LANES), 1)
    n_sub = q_ref.shape[1] // ATTN_SUB

    def scores(r, m):
        q = q_ref[0, r * ATTN_SUB:(r + 1) * ATTN_SUB, :]
        qm = jnp.where((lane < D_C) if m == 0 else (lane >= D_C), q, jnp.zeros_like(q))
        return _dot_nt(qm, k)

    def weighted(s):
        p = jnp.exp2(s - jnp.max(s, axis=-1, keepdims=True))
        return _dot(p.astype(BF16), v) / jnp.sum(p, axis=-1, keepdims=True)

    units = [(r, m) for r in range(n_sub) for m in range(2)]
    outs = {}
    pending = [scores(*u) for u in units[:ATTN_AHEAD]]
    for idx, unit in enumerate(units):
        if idx + ATTN_AHEAD < len(units):
            pending.append(scores(*units[idx + ATTN_AHEAD]))
        outs[unit] = weighted(pending.pop(0))
    for r in range(n_sub):
        rows = slice(r * ATTN_SUB, (r + 1) * ATTN_SUB)
        o = outs[(r, 0)] - lam * outs[(r, 1)]
        y = o * lax.rsqrt(jnp.mean(o * o, axis=-1, keepdims=True) + EPS) * g_ref[...]
        z = z_ref[0, rows, :].astype(F32)
        o_ref[0, rows, :] = (y * out_scale * (z * jax.nn.sigmoid(z))).astype(BF16)


def _attn(scal, qn, kn, proj, subln_g, tq=2048):
    b, s, _ = proj.shape
    tq = min(tq, s)
    assert tq % ATTN_SUB == 0 and s % tq == 0
    return pl.pallas_call(
        _attn_body,
        grid=(b, N_HEADS, s // tq),
        in_specs=[
            pl.BlockSpec(memory_space=pltpu.SMEM),
            pl.BlockSpec((1, tq, LANES), lambda bi, h, qi: (bi, qi, h)),
            pl.BlockSpec((1, s, LANES), lambda bi, h, qi: (bi, 0, h)),
            pl.BlockSpec((1, s, LANES), lambda bi, h, qi: (bi, 0, CB_CV + h)),
            pl.BlockSpec((1, tq, LANES), lambda bi, h, qi: (bi, qi, CB_CZ + h)),
            pl.BlockSpec((1, LANES), lambda bi, h, qi: (0, 0)),
        ],
        out_specs=pl.BlockSpec((1, tq, LANES), lambda bi, h, qi: (bi, qi, h)),
        out_shape=jax.ShapeDtypeStruct((b, s, N_HEADS * HEAD_W), BF16),
        compiler_params=_cparams(("parallel", "parallel", "arbitrary")),
        name="diffattn",
    )(scal, qn, kn, proj, proj, subln_g)


def _merge_body(x_ref, ya_ref, yb_ref, yc_ref, g0_ref, g1_ref, g2_ref, wa_ref, wb_ref, wc_ref, wo_ref, o_ref):
    def branch(y_ref, w_ref, g_ref):
        return jax.nn.sigmoid(g_ref[...].astype(F32)) * jnp.dot(y_ref[...], w_ref[...], preferred_element_type=F32)

    merged = branch(ya_ref, wa_ref, g0_ref) + branch(yb_ref, wb_ref, g1_ref) + branch(yc_ref, wc_ref, g2_ref)
    o_ref[...] = x_ref[...] + jnp.dot(merged.astype(BF16), wo_ref[...], preferred_element_type=F32)


def _merge(x2d, ya, yb, yc, proj2d, wa, wb, wc, wo, row0=0, n_rows=None, tm=512):
    n_rows = x2d.shape[0] if n_rows is None else n_rows
    assert row0 % tm == 0 and n_rows % tm == 0
    t0 = row0 // tm
    w_br = N_HEADS * HEAD_W
    row = lambda i: (t0 + i, 0)
    full = lambda i: (0, 0)
    y_spec = pl.BlockSpec((tm, w_br), row)
    wbr_spec = pl.BlockSpec((w_br, D_MODEL), full)
    return pl.pallas_call(
        _merge_body,
        grid=(n_rows // tm,),
        in_specs=[
            pl.BlockSpec((tm, D_MODEL), row),
            y_spec, y_spec, y_spec,
            pl.BlockSpec((tm, D_MODEL), lambda i: (t0 + i, 0)),
            pl.BlockSpec((tm, D_MODEL), lambda i: (t0 + i, 1)),
            pl.BlockSpec((tm, D_MODEL), lambda i: (t0 + i, 2)),
            wbr_spec, wbr_spec, wbr_spec,
            pl.BlockSpec((D_MODEL, D_MODEL), full),
        ],
        out_specs=pl.BlockSpec((tm, D_MODEL), lambda i: (i, 0)),
        out_shape=jax.ShapeDtypeStruct((n_rows, D_MODEL), F32),
        compiler_params=_cparams(("parallel",)),
        name="merge",
    )(x2d, ya, yb, yc, proj2d, proj2d, proj2d, wa, wb, wc, wo)


def _split3(x):
    p1 = x.astype(BF16)
    r1 = x - p1.astype(F32)
    p2 = r1.astype(BF16)
    p3 = (r1 - p2.astype(F32)).astype(BF16)
    return p1, p2, p3


def _softplus(x):
    return jnp.maximum(x, 0.0) + jnp.log1p(jnp.exp(-jnp.abs(x)))


def _silu(x):
    return x * jax.nn.sigmoid(x)


def _dot(a, b):
    return jnp.dot(a, b, preferred_element_type=F32)


def _dot_nt(a, b):
    return lax.dot_general(a, b, (((1,), (1,)), ((), ())), preferred_element_type=F32)


CUM_TILE = 256
GDN_GROUP = 8


def _gdn_body(alog_ref, dtb_ref, q_ref, k_ref, v_ref, z_ref, cwq_ref, cwk_ref, cwv_ref, sm_ref, smt_ref,
              nav_ref, dtv_ref, ng_ref, o_ref,
              xp_ref, qn_ref, kn_ref, vn_ref, gcc_ref, bc_ref, gcr_ref, br_ref,
              lhs_ref, r_ref, gl_ref, of_ref, ob_ref):
    h = pl.program_id(1)
    seq = q_ref.shape[1]
    c = CHUNK_A
    n_chunks = seq // c
    dk = HEAD_W

    row_t = lax.broadcasted_iota(jnp.int32, (CUM_TILE, CUM_TILE), 0)
    col_t = lax.broadcasted_iota(jnp.int32, (CUM_TILE, CUM_TILE), 1)
    same_chunk = (row_t // c) == (col_t // c)
    tri = (jnp.where(same_chunk & (col_t <= row_t), 1.0, 0.0).astype(BF16),
           jnp.where(same_chunk & (col_t >= row_t), 1.0, 0.0).astype(BF16))
    lane_row = lax.broadcasted_iota(jnp.int32, (1, LANES), 1)
    sel_src = lax.broadcasted_iota(jnp.int32, (LANES, 4 * LANES), 0)
    sel_blk = lax.broadcasted_iota(jnp.int32, (LANES, 4 * LANES), 1) // LANES
    sel_want = jnp.where(sel_blk < 2, 2 * N_HEADS + N_HEADS * sel_blk, N_HEADS * (sel_blk - 2)) + h
    sel4 = jnp.where(sel_src == sel_want, 1.0, 0.0).astype(BF16)

    def gates_tile(t, carry):
        rows = pl.ds(pl.multiple_of(t * CUM_TILE, CUM_TILE), CUM_TILE)
        sm = sm_ref[0, rows, :]
        gates = jnp.where(lane_row < 2 * N_HEADS, jax.nn.sigmoid(sm), nav_ref[...] * _softplus(sm + dtv_ref[...]))
        rep = _dot(jnp.concatenate(_split3(gates), axis=0), sel4)
        piece = lambda p, blk: rep[p * CUM_TILE:(p + 1) * CUM_TILE, blk * LANES:(blk + 1) * LANES]
        for d in range(2):
            cs = _dot(tri[d], jnp.concatenate([piece(p, d) for p in range(3)], axis=1).astype(BF16))
            gcc_ref[d, rows, :] = cs[:, :LANES] + cs[:, LANES:2 * LANES] + cs[:, 2 * LANES:]
            bc_ref[d, rows, :] = piece(0, 2 + d) + piece(1, 2 + d) + piece(2, 2 + d)
        return carry

    lax.fori_loop(0, seq // CUM_TILE, gates_tile, 0, unroll=4)

    rc = lax.broadcasted_iota(jnp.int32, (c, c), 0)
    cc = lax.broadcasted_iota(jnp.int32, (c, c), 1)
    for d in range(2):
        a_neg = -jnp.exp(alog_ref[d, h])
        g = a_neg * _softplus(smt_ref[0, 8 + 4 * d + h] + dtb_ref[d, h])
        cum = jnp.where((rc <= cc) if d == 0 else (rc >= cc), 1.0, 0.0).astype(BF16)
        acc = jnp.zeros((n_chunks, c), F32)
        for part in _split3(g):
            acc = acc + _dot(part, cum)
        gcr_ref[d] = acc
        br_ref[d] = jax.nn.sigmoid(smt_ref[0, 4 * d + h])

    pad = 8
    zeros_pad = jnp.zeros((pad, LANES), F32)

    def conv_into(src_ref, cw_ref, dst_ref, normalise, scale):
        xp_ref[pl.ds(0, pad), :] = zeros_pad
        xp_ref[pl.ds(pad + seq, pad), :] = zeros_pad
        xp_ref[pl.ds(pad, seq), :] = src_ref[0].astype(F32)
        cw = cw_ref[...]

        def chunk(n, carry):
            base = pl.multiple_of(n * c, c)
            y = jnp.zeros((c, LANES), F32)
            for j in range(CONV_K):
                off = pad - (CONV_K - 1) // 2 + j
                y = y + xp_ref[pl.ds(base + off, c), :] * cw[j:j + 1, :]
            y = _silu(y)
            if normalise:
                y = y * lax.rsqrt(jnp.sum(y * y, axis=-1, keepdims=True) + EPS)
            dst_ref[pl.ds(base, c), :] = y * scale if scale != 1.0 else y
            return carry

        lax.fori_loop(0, n_chunks, chunk, 0, unroll=8)

    conv_into(q_ref, cwq_ref, qn_ref, True, dk ** -0.5)
    conv_into(k_ref, cwk_ref, kn_ref, True, 1.0)
    conv_into(v_ref, cwv_ref, vn_ref, False, 1.0)

    eye = jnp.where(rc == cc, 1.0, 0.0)
    masks = (((rc >= cc), (rc > cc)), ((rc <= cc), (rc < cc)))

    def prep_group(i, carry):
        chains = []
        for g in range(GDN_GROUP):
            n = i * GDN_GROUP + g
            rows = pl.ds(pl.multiple_of(n * c, c), c)
            q = qn_ref[rows, :]
            k = kn_ref[rows, :]
            k16 = k.astype(BF16)
            v16 = vn_ref[rows, :].astype(BF16)
            kq = _dot_nt(jnp.concatenate([k16, q.astype(BF16)], axis=0), k16)
            kk, qk = kq[:c, :], kq[c:, :]
            for d in range(2):
                incl, strict = masks[d]
                gcol = gcc_ref[d, rows, :]
                grow = gcr_ref[d, pl.ds(n, 1), :]
                decay = jnp.where(incl, jnp.exp(jnp.where(incl, gcol[:, :c] - grow, 0.0)), 0.0)
                x = -jnp.where(strict, kk * decay * bc_ref[d, rows, :][:, :c], 0.0)
                last = (c - 1) if d == 0 else 0
                g_last = gcol[last:last + 1, :]
                gl_ref[d, pl.ds(n, 1), :] = jnp.exp(g_last)
                chains.append(dict(d=d, n=n, rows=rows, x=x, grow=grow, k16=k16, v16=v16,
                                   a16=(qk * decay).astype(BF16), qd=q * jnp.exp(gcol),
                                   kdt16=(k * jnp.exp(g_last - gcol)).T.astype(BF16)))
        t_inv = [eye + ch["x"] for ch in chains]
        pw16 = [ch["x"].astype(BF16) for ch in chains]
        pw16 = [_dot(p, p).astype(BF16) for p in pw16]
        for _ in range(4):
            both = [_dot(jnp.concatenate([t.astype(BF16), p], axis=0), p) for t, p in zip(t_inv, pw16)]
            t_inv = [t + r[:c, :] for t, r in zip(t_inv, both)]
            pw16 = [r[c:, :].astype(BF16) for r in both]
        t_inv = [t + _dot(t.astype(BF16), p) for t, p in zip(t_inv, pw16)]
        wu = []
        for ch, t in zip(chains, t_inv):
            tb = t * br_ref[ch["d"], pl.ds(ch["n"], 1), :]
            res = _dot(jnp.concatenate([(tb * jnp.exp(ch["grow"])).astype(BF16), tb.astype(BF16)], axis=0),
                       jnp.concatenate([ch["k16"], ch["v16"]], axis=1))
            wu.append(jnp.concatenate([res[:c, :dk], res[c:, dk:]], axis=1).astype(BF16))
        for ch, wu16 in zip(chains, wu):
            d, n = ch["d"], ch["n"]
            res = _dot(jnp.concatenate([ch["kdt16"], ch["a16"]], axis=0), wu16)
            lhs_ref[d, n, pl.ds(0, dk), :] = res[:dk, :dk].astype(BF16)
            lhs_ref[d, n, pl.ds(dk, c), :] = (ch["qd"] - res[dk:, :dk]).astype(BF16)
            r_ref[d, n] = res[:dk, dk:].astype(BF16)
            (of_ref if d == 0 else ob_ref)[ch["rows"], :] = res[dk:, dk:]
        return carry

    lax.fori_loop(0, n_chunks // GDN_GROUP, prep_group, 0)

    def scan_step(i, carry):
        outs = []
        for d, s, o_out in ((0, carry[0], of_ref), (1, carry[1], ob_ref)):
            n = i if d == 0 else n_chunks - 1 - i
            rows = pl.ds(pl.multiple_of(n * c, c), c)
            res = _dot(lhs_ref[d, n], s.astype(BF16))
            o_out[rows, :] = o_out[rows, :] + res[dk:, :]
            outs.append(s * gl_ref[d, pl.ds(n, 1), :] + r_ref[d, n].astype(F32) - res[:dk, :])
        return tuple(outs)

    s0 = jnp.zeros((dk, HEAD_W), F32)
    lax.fori_loop(0, n_chunks, scan_step, (s0, s0))

    def finish(t, carry):
        rows = pl.ds(pl.multiple_of(t * CUM_TILE, CUM_TILE), CUM_TILE)
        o = of_ref[rows, :] + ob_ref[rows, :]
        y = o * lax.rsqrt(jnp.mean(o * o, axis=-1, keepdims=True) + EPS) * ng_ref[...]
        o_ref[0, rows, :] = (y * _silu(z_ref[0, rows, :].astype(F32))).astype(BF16)
        return carry

    lax.fori_loop(0, seq // CUM_TILE, finish, 0, unroll=2)


def _mixer_a(proj, small, small_t, conv_w, a_log, dt_bias, neg_a_vec, dtb_vec, norm_g):
    b, s, _ = proj.shape
    n_chunks = s // CHUNK_A
    assert s % CUM_TILE == 0 and n_chunks % GDN_GROUP == 0
    col = lambda cb: pl.BlockSpec((1, s, LANES), lambda bi, h: (bi, 0, cb + h))
    cw = lambda cb: pl.BlockSpec((CONV_K, LANES), lambda bi, h: (0, cb + h))
    vec = pl.BlockSpec((1, LANES), lambda bi, h: (0, 0))
    smem = pl.BlockSpec(memory_space=pltpu.SMEM)
    return pl.pallas_call(
        _gdn_body,
        grid=(b, N_HEADS),
        in_specs=[
            smem, smem,
            col(CB_AQ), col(CB_AK), col(CB_AV), col(CB_AZ),
            cw(0), cw(4), cw(8),
            pl.BlockSpec((1, s, LANES), lambda bi, h: (bi, 0, 0)),
            pl.BlockSpec((1, N_SMALL, n_chunks, CHUNK_A), lambda bi, h: (bi, 0, 0, 0)),
            vec, vec, vec,
        ],
        out_specs=pl.BlockSpec((1, s, LANES), lambda bi, h: (bi, 0, h)),
        out_shape=jax.ShapeDtypeStruct((b, s, N_HEADS * HEAD_W), BF16),
        scratch_shapes=[
            pltpu.VMEM((s + 16, LANES), F32),
            pltpu.VMEM((s, LANES), F32),
            pltpu.VMEM((s, LANES), F32),
            pltpu.VMEM((s, LANES), F32),
            pltpu.VMEM((2, s, LANES), F32),
            pltpu.VMEM((2, s, LANES), F32),
            pltpu.VMEM((2, n_chunks, CHUNK_A), F32),
            pltpu.VMEM((2, n_chunks, CHUNK_A), F32),
            pltpu.VMEM((2, n_chunks, HEAD_W + CHUNK_A, LANES), BF16),
            pltpu.VMEM((2, n_chunks, HEAD_W, LANES), BF16),
            pltpu.VMEM((2, n_chunks, LANES), F32),
            pltpu.VMEM((s, LANES), F32),
            pltpu.VMEM((s, LANES), F32),
        ],
        compiler_params=_cparams(("parallel", "arbitrary")),
        name="gdn",
    )(a_log, dt_bias, proj, proj, proj, proj, conv_w, conv_w, conv_w, small, small_t, neg_a_vec, dtb_vec, norm_g)


HGRN_GROUP = 16


def _hgrn_body(q_ref, i_ref, ff_ref, fb_ref, z_ref, lb_ref, ng_ref, o_ref,
               qs_ref, kk_ref, gc_ref, gt_ref, qd_ref, kd_ref, oi_ref, of_ref, ob_ref):
    seq = q_ref.shape[1]
    c = CHUNK_B
    n_chunks = seq // c
    dk = HEAD_W

    row_t = lax.broadcasted_iota(jnp.int32, (CUM_TILE, CUM_TILE), 0)
    col_t = lax.broadcasted_iota(jnp.int32, (CUM_TILE, CUM_TILE), 1)
    same_chunk = (row_t // c) == (col_t // c)
    tri = (jnp.where(same_chunk & (col_t <= row_t), 1.0, 0.0).astype(BF16),
           jnp.where(same_chunk & (col_t >= row_t), 1.0, 0.0).astype(BF16))

    def gates_tile(t, carry):
        rows = pl.ds(pl.multiple_of(t * CUM_TILE, CUM_TILE), CUM_TILE)
        qs = _silu(q_ref[0, rows, :].astype(F32)) * (dk ** -0.5)
        qs_ref[rows, :] = qs
        for d, f_ref in ((0, ff_ref), (1, fb_ref)):
            bf = f_ref[0, rows, :].astype(F32)
            kk = (1.0 - lb_ref[d:d + 1, :]) * jax.nn.sigmoid(-bf)
            log_f = jnp.log1p(-jnp.minimum(kk, 1.0 - F_MIN_GAP))
            gc = jnp.zeros((CUM_TILE, LANES), F32)
            for part in _split3(log_f):
                gc = gc + _dot(tri[d], part)
            end = (c - 1) if d == 0 else 0
            gt = jnp.concatenate([jnp.broadcast_to(gc[k * c + end:k * c + end + 1, :], (c, LANES))
                                  for k in range(CUM_TILE // c)], axis=0)
            kk_ref[d, rows, :] = kk
            gc_ref[d, rows, :] = gc * LOG2E
            gt_ref[d, rows, :] = gt
            qd_ref[d, rows, :] = (qs * jnp.exp(gc)).astype(BF16)
            kd_ref[d, rows, :] = (kk * jnp.exp(gt - gc)).astype(BF16)
        return carry

    lax.fori_loop(0, seq // CUM_TILE, gates_tile, 0, unroll=4)

    jrow = lax.broadcasted_iota(jnp.int32, (SUB, LANES), 0)
    ones_w = jnp.ones((LANES, LANES), BF16)
    n_pieces = c + c // 2
    piece = lax.broadcasted_iota(jnp.int32, (c, n_pieces * SUB), 1) // SUB
    out_row = lax.broadcasted_iota(jnp.int32, (c, n_pieces * SUB), 0)
    row_of_piece = (jnp.where(piece < SUB, piece, SUB + (piece - SUB) // 2),
                    jnp.where(piece < c, piece // 2, piece - SUB))
    sel_both = jnp.concatenate([jnp.where(rp == out_row, 1.0, 0.0) for rp in row_of_piece], axis=1).astype(BF16)

    def scan_step(i, carry):
        tiles, vstacks, row_sets = [], [], []
        for g in range(HGRN_GROUP):
            rows = pl.ds(pl.multiple_of((i * HGRN_GROUP + g) * c, c), c)
            qs = qs_ref[rows, :]
            v = i_ref[0, rows, :].astype(F32)
            parts, vparts = [], []
            for d in range(2):
                kk = kk_ref[d, rows, :]
                gc2 = gc_ref[d, rows, :]
                for r in range(c):
                    for j0 in (0, SUB):
                        if (j0 > r) if d == 0 else (j0 + SUB - 1 < r):
                            continue
                        pair = jnp.exp2(gc2[r:r + 1, :] - gc2[j0:j0 + SUB, :])
                        if not ((j0 + SUB - 1 <= r) if d == 0 else (j0 >= r)):
                            mask = (jrow + j0 <= r) if d == 0 else (jrow + j0 >= r)
                            pair = jnp.where(mask, pair, 0.0)
                        parts.append(pair * kk[j0:j0 + SUB, :] * qs[r:r + 1, :])
                        vparts.append(v[j0:j0 + SUB, :])
            tiles.append(jnp.concatenate(parts, axis=0).astype(BF16))
            vstacks.append(jnp.concatenate(vparts, axis=0))
            row_sets.append(rows)
        a_reps = [_dot(t, ones_w) for t in tiles]
        upds = {}
        for g in range(HGRN_GROUP):
            for d in range(2):
                n = i * HGRN_GROUP + g
                n = n if d == 0 else n_chunks - 1 - n
                rows = pl.ds(pl.multiple_of(n * c, c), c)
                upds[(g, d)] = (rows, lax.dot_general(i_ref[0, rows, :], kd_ref[d, rows, :], (((0,), (0,)), ((), ())),
                                                      preferred_element_type=F32))
        for rows, a_rep, v_stack in zip(row_sets, a_reps, vstacks):
            oi_ref[rows, :] = _dot(sel_both, (a_rep * v_stack).astype(BF16))
        st = list(carry)
        for g in range(HGRN_GROUP):
            for d in range(2):
                rows, upd = upds[(g, d)]
                (of_ref if d == 0 else ob_ref)[rows, :] = _dot_nt(qd_ref[d, rows, :], st[d].astype(BF16))
                e_row = jnp.exp(gt_ref[d, pl.ds(rows.start, 1), :])
                st[d] = st[d] * e_row + upd
        return tuple(st)

    s0 = jnp.zeros((HEAD_W, dk), F32)
    lax.fori_loop(0, n_chunks // HGRN_GROUP, scan_step, (s0, s0))

    def finish(t, carry):
        rows = pl.ds(pl.multiple_of(t * CUM_TILE, CUM_TILE), CUM_TILE)
        o = oi_ref[rows, :] + of_ref[rows, :] + ob_ref[rows, :]
        y = o * lax.rsqrt(jnp.mean(o * o, axis=-1, keepdims=True) + EPS) * ng_ref[...]
        o_ref[0, rows, :] = (y * _silu(z_ref[0, rows, :].astype(F32))).astype(BF16)
        return carry

    lax.fori_loop(0, seq // CUM_TILE, finish, 0, unroll=2)


def _mixer_b(proj, lb, norm_g):
    b, s, _ = proj.shape
    assert s % CUM_TILE == 0 and (s // CHUNK_B) % HGRN_GROUP == 0
    col = lambda cb: pl.BlockSpec((1, s, LANES), lambda bi, h: (bi, 0, cb + h))
    return pl.pallas_call(
        _hgrn_body,
        grid=(b, N_HEADS),
        in_specs=[
            col(CB_BQ), col(CB_BI), col(CB_BF), col(CB_BF + N_HEADS), col(CB_BZ),
            pl.BlockSpec((2, LANES), lambda bi, h: (0, h)),
            pl.BlockSpec((1, LANES), lambda bi, h: (0, 0)),
        ],
        out_specs=pl.BlockSpec((1, s, LANES), lambda bi, h: (bi, 0, h)),
        out_shape=jax.ShapeDtypeStruct((b, s, N_HEADS * HEAD_W), BF16),
        scratch_shapes=[
            pltpu.VMEM((s, LANES), F32),
            pltpu.VMEM((2, s, LANES), F32),
            pltpu.VMEM((2, s, LANES), F32),
            pltpu.VMEM((2, s, LANES), F32),
            pltpu.VMEM((2, s, LANES), BF16),
            pltpu.VMEM((2, s, LANES), BF16),
            pltpu.VMEM((s, LANES), F32),
            pltpu.VMEM((s, LANES), F32),
            pltpu.VMEM((s, LANES), F32),
        ],
        compiler_params=_cparams(("parallel", "arbitrary")),
        name="hgrn",
    )(proj, proj, proj, proj, proj, lb, norm_g)


def _rope_tables(seq):
    half = ROPE_DIM // 2
    inv = 1.0 / (ROPE_THETA ** (jnp.arange(0, ROPE_DIM, 2, dtype=F32) / ROPE_DIM))
    ang = jnp.arange(seq, dtype=F32)[:, None] * inv[None, :]
    cos, sin = jnp.cos(ang), jnp.sin(ang)
    one = jnp.ones((seq, D_C - ROPE_DIM), F32)
    zero = jnp.zeros((seq, D_C - ROPE_DIM), F32)
    zh = jnp.zeros((seq, half), F32)
    c_map = jnp.concatenate([cos, cos, one], axis=1)
    s1_map = jnp.concatenate([zh, sin, zero], axis=1)
    s2_map = jnp.concatenate([-sin, zh, zero], axis=1)
    tile2 = lambda t: jnp.concatenate([t, t], axis=1)
    return tile2(c_map), tile2(s1_map), tile2(s2_map)


def _lower_bounds(lb_logits):
    p = jax.nn.softmax(lb_logits.astype(F32), axis=1)
    return jnp.cumsum(p, axis=1) - p[:, :1]


def kernel(x_prompt, x_sample, norm_g, w_in, conv_w, a_log, dt_bias, gdn_norm_g, hgrn_lb_logits, hgrn_norm_g,
           q_norm_g, k_norm_g, diff_lambda, subln_g, w_br_a, w_br_b, w_br_c, w_out):
    nb_p = x_prompt.shape[0]
    x = jnp.concatenate([x_prompt, x_sample], axis=0)
    b, s, d = x.shape
    depth = w_in.shape[0]
    n_in = w_in.shape[-1]
    gate0 = n_in - 3 * D_MODEL
    w_main = jnp.concatenate([w_in[:, :, gate0:], w_in[:, :, :2048], w_in[:, :, 2048 + N_SMALL:gate0]],
                             axis=-1).astype(BF16)
    w_small = jnp.pad(w_in[:, :, 2048:2048 + N_SMALL], ((0, 0), (0, 0), (0, LANES - N_SMALL))).astype(BF16)
    wa, wb, wc, wo = (w.astype(BF16) for w in (w_br_a, w_br_b, w_br_c, w_out))
    cos_t, s1_t, s2_t = _rope_tables(s)
    lbs = _lower_bounds(hgrn_lb_logits)
    lane_pad = ((0, 0), (2 * N_HEADS, LANES - 4 * N_HEADS))
    neg_a_vec = jnp.pad(-jnp.exp(a_log.astype(F32)).reshape(depth, 2 * N_HEADS), lane_pad)[:, None, :]
    dtb_vec = jnp.pad(dt_bias.astype(F32).reshape(depth, 2 * N_HEADS), lane_pad)[:, None, :]
    lp = diff_lambda.astype(F32)
    lam_dyn = jnp.exp(jnp.sum(lp[:, 0] * lp[:, 1], axis=-1)) - jnp.exp(jnp.sum(lp[:, 2] * lp[:, 3], axis=-1))

    x2d = x.reshape(b * s, d)
    for l in range(depth):
        lambda_init = 0.8 - 0.6 * math.exp(-0.3 * l)
        proj2d, small2d = _inproj(x2d, norm_g[l][None, :], w_main[l], w_small[l])
        proj = proj2d.reshape(b, s, N_MAIN)
        small = small2d.reshape(b, s, LANES)
        small_t = small[:, :, :N_SMALL].transpose(0, 2, 1).reshape(b, N_SMALL, s // CHUNK_A, CHUNK_A)
        ya = _mixer_a(proj, small, small_t, conv_w[l], a_log[l], dt_bias[l], neg_a_vec[l], dtb_vec[l],
                      gdn_norm_g[l][None, :])
        yb = _mixer_b(proj, lbs[:, l], hgrn_norm_g[l][None, :])
        qn, kn = _qkprep(proj, cos_t, s1_t, s2_t, q_norm_g[l].reshape(1, LANES), k_norm_g[l].reshape(1, LANES))
        scal = jnp.stack([lam_dyn[l] + lambda_init, jnp.asarray(1.0 - lambda_init, F32)]).astype(F32)
        yc = _attn(scal, qn, kn, proj, subln_g[l][None, :])
        merge_args = (x2d, ya.reshape(b * s, -1), yb.reshape(b * s, -1), yc.reshape(b * s, -1), proj2d,
                      wa[l], wb[l], wc[l], wo[l])
        if l + 1 < depth:
            x2d = _merge(*merge_args)
    y_prompt = _merge(*merge_args, row0=0, n_rows=nb_p * s)
    y_sample = _merge(*merge_args, row0=nb_p * s, n_rows=(b - nb_p) * s)
    return (y_prompt.reshape(nb_p, s, d), y_sample.reshape(b - nb_p, s, d))
```

```python
import functools
import math

import jax
import jax.numpy as jnp
from jax import lax
from jax.experimental import pallas as pl
from jax.experimental.pallas import tpu as pltpu

F32 = jnp.float32
BF16 = jnp.bfloat16
EPS = 1e-6
F_MIN_GAP = 1e-6
LOG2E = math.log2(math.e)

D_MODEL = 1024
LANES = 128
SUB = 8
N_HEADS = 4
HEAD_W = 128
CONV_K = 5
CHUNK_A = 64
CHUNK_B = 16
D_C = 64
ROPE_DIM = D_C // 4
ROPE_THETA = 500000.0
N_SMALL = 16
N_MAIN = 9728
CB_AQ, CB_AK, CB_AV, CB_AZ = 24, 28, 32, 36
CB_BQ, CB_BI, CB_BF, CB_BZ = 40, 44, 48, 56
CB_CQ, CB_CK, CB_CV, CB_CZ = 60, 64, 68, 72
VMEM_LIMIT = 56 * 1024 * 1024


def _cparams(sem):
    return pltpu.CompilerParams(dimension_semantics=sem, vmem_limit_bytes=VMEM_LIMIT)


def _inproj_body(x_ref, g_ref, w_ref, ws_ref, o_ref, os_ref, h_ref):
    @pl.when(pl.program_id(1) == 0)
    def _():
        x = x_ref[...]
        h = x * lax.rsqrt(jnp.mean(x * x, axis=-1, keepdims=True) + EPS) * g_ref[...]
        hb = h.astype(BF16)
        h_ref[...] = hb
        os_ref[...] = jnp.dot(hb, ws_ref[...], preferred_element_type=F32)

    o_ref[...] = jnp.dot(h_ref[...], w_ref[...], preferred_element_type=F32).astype(BF16)


def _inproj(x2d, g, w_main, w_small, tm=1024, tn=2432):
    t = x2d.shape[0]
    while t % tm:
        tm //= 2
    return pl.pallas_call(
        _inproj_body,
        grid=(t // tm, N_MAIN // tn),
        in_specs=[
            pl.BlockSpec((tm, D_MODEL), lambda i, j: (i, 0)),
            pl.BlockSpec((1, D_MODEL), lambda i, j: (0, 0)),
            pl.BlockSpec((D_MODEL, tn), lambda i, j: (0, j)),
            pl.BlockSpec((D_MODEL, LANES), lambda i, j: (0, 0)),
        ],
        out_specs=[
            pl.BlockSpec((tm, tn), lambda i, j: (i, j)),
            pl.BlockSpec((tm, LANES), lambda i, j: (i, 0)),
        ],
        out_shape=[
            jax.ShapeDtypeStruct((t, N_MAIN), BF16),
            jax.ShapeDtypeStruct((t, LANES), F32),
        ],
        scratch_shapes=[pltpu.VMEM((tm, D_MODEL), BF16)],
        compiler_params=_cparams(("parallel", "arbitrary")),
        name="inproj",
    )(x2d, g, w_main, w_small)


def _qkprep_body(q_ref, k_ref, c_ref, s1_ref, s2_ref, qg_ref, kg_ref, qo_ref, ko_ref, *, q_scale):
    cos, s1, s2 = c_ref[...], s1_ref[...], s2_ref[...]
    same_map = jnp.where(lax.broadcasted_iota(jnp.int32, (LANES, LANES), 0) // D_C
                         == lax.broadcasted_iota(jnp.int32, (LANES, LANES), 1) // D_C, 1.0, 0.0).astype(BF16)

    def prep(x, g):
        x = x.astype(F32)
        x2 = x * x
        x2_hi = x2.astype(BF16)
        x2_lo = (x2 - x2_hi.astype(F32)).astype(BF16)
        ms = (_dot(x2_hi, same_map) + _dot(x2_lo, same_map)) * (1.0 / D_C)
        y = x * lax.rsqrt(ms + EPS) * g
        return y * cos + pltpu.roll(y, ROPE_DIM // 2, 1) * s1 + pltpu.roll(y, LANES - ROPE_DIM // 2, 1) * s2

    qo_ref[0] = (prep(q_ref[0], qg_ref[...]) * q_scale).astype(BF16)
    ko_ref[0] = prep(k_ref[0], kg_ref[...]).astype(BF16)


def _qkprep(proj, cos_t, s1_t, s2_t, qg, kg, ts=1024):
    b, s, _ = proj.shape
    ts = min(ts, s)
    q_scale = (D_C ** -0.5) * math.log2(math.e)
    tab = pl.BlockSpec((ts, LANES), lambda bi, si, h: (si, 0))
    vec = pl.BlockSpec((1, LANES), lambda bi, si, h: (0, 0))
    out = pl.BlockSpec((1, ts, LANES), lambda bi, si, h: (bi, si, h))
    return pl.pallas_call(
        functools.partial(_qkprep_body, q_scale=q_scale),
        grid=(b, s // ts, N_HEADS),
        in_specs=[
            pl.BlockSpec((1, ts, LANES), lambda bi, si, h: (bi, si, CB_CQ + h)),
            pl.BlockSpec((1, ts, LANES), lambda bi, si, h: (bi, si, CB_CK + h)),
            tab, tab, tab, vec, vec,
        ],
        out_specs=[out, out],
        out_shape=[jax.ShapeDtypeStruct((b, s, N_HEADS * HEAD_W), BF16)] * 2,
        compiler_params=_cparams(("parallel", "parallel", "parallel")),
        name="qkprep",
    )(proj, proj, cos_t, s1_t, s2_t, qg, kg)


ATTN_SUB = 256
ATTN_AHEAD = 1


def _attn_body(sc_ref, q_ref, k_ref, v_ref, z_ref, g_ref, o_ref):
    lam = sc_ref[0]
    out_scale = sc_ref[1]
    k = k_ref[0]
    v = v_ref[0]
    lane = lax.broadcasted_iota(jnp.int32, (1, LANES), 1)
    n_sub = q_ref.shape[1] // ATTN_SUB

    def scores(r, m):
        q = q_ref[0, r * ATTN_SUB:(r + 1) * ATTN_SUB, :]
        qm = jnp.where((lane < D_C) if m == 0 else (lane >= D_C), q, jnp.zeros_like(q))
        return _dot_nt(qm, k)

    def weighted(s):
        p = jnp.exp2(s - jnp.max(s, axis=-1, keepdims=True))
        return _dot(p.astype(BF16), v) / jnp.sum(p, axis=-1, keepdims=True)

    units = [(r, m) for r in range(n_sub) for m in range(2)]
    outs = {}
    pending = [scores(*u) for u in units[:ATTN_AHEAD]]
    for idx, unit in enumerate(units):
        if idx + ATTN_AHEAD < len(units):
            pending.append(scores(*units[idx + ATTN_AHEAD]))
        outs[unit] = weighted(pending.pop(0))
    for r in range(n_sub):
        rows = slice(r * ATTN_SUB, (r + 1) * ATTN_SUB)
        o = outs[(r, 0)] - lam * outs[(r, 1)]
        y = o * lax.rsqrt(jnp.mean(o * o, axis=-1, keepdims=True) + EPS) * g_ref[...]
        z = z_ref[0, rows, :].astype(F32)
        o_ref[0, rows, :] = (y * out_scale * (z * jax.nn.sigmoid(z))).astype(BF16)


def _attn(scal, qn, kn, proj, subln_g, tq=2048):
    b, s, _ = proj.shape
    tq = min(tq, s)
    assert tq % ATTN_SUB == 0 and s % tq == 0
    return pl.pallas_call(
        _attn_body,
        grid=(b, N_HEADS, s // tq),
        in_specs=[
            pl.BlockSpec(memory_space=pltpu.SMEM),
            pl.BlockSpec((1, tq, LANES), lambda bi, h, qi: (bi, qi, h)),
            pl.BlockSpec((1, s, LANES), lambda bi, h, qi: (bi, 0, h)),
            pl.BlockSpec((1, s, LANES), lambda bi, h, qi: (bi, 0, CB_CV + h)),
            pl.BlockSpec((1, tq, LANES), lambda bi, h, qi: (bi, qi, CB_CZ + h)),
            pl.BlockSpec((1, LANES), lambda bi, h, qi: (0, 0)),
        ],
        out_specs=pl.BlockSpec((1, tq, LANES), lambda bi, h, qi: (bi, qi, h)),
        out_shape=jax.ShapeDtypeStruct((b, s, N_HEADS * HEAD_W), BF16),
        compiler_params=_cparams(("parallel", "parallel", "arbitrary")),
        name="diffattn",
    )(scal, qn, kn, proj, proj, subln_g)


def _merge_body(x_ref, ya_ref, yb_ref, yc_ref, g0_ref, g1_ref, g2_ref, wa_ref, wb_ref, wc_ref, wo_ref, o_ref):
    def branch(y_ref, w_ref, g_ref):
        return jax.nn.sigmoid(g_ref[...].astype(F32)) * jnp.dot(y_ref[...], w_ref[...], preferred_element_type=F32)

    merged = branch(ya_ref, wa_ref, g0_ref) + branch(yb_ref, wb_ref, g1_ref) + branch(yc_ref, wc_ref, g2_ref)
    o_ref[...] = x_ref[...] + jnp.dot(merged.astype(BF16), wo_ref[...], preferred_element_type=F32)


def _merge(x2d, ya, yb, yc, proj2d, wa, wb, wc, wo, row0=0, n_rows=None, tm=512):
    n_rows = x2d.shape[0] if n_rows is None else n_rows
    assert row0 % tm == 0 and n_rows % tm == 0
    t0 = row0 // tm
    w_br = N_HEADS * HEAD_W
    row = lambda i: (t0 + i, 0)
    full = lambda i: (0, 0)
    y_spec = pl.BlockSpec((tm, w_br), row)
    wbr_spec = pl.BlockSpec((w_br, D_MODEL), full)
    return pl.pallas_call(
        _merge_body,
        grid=(n_rows // tm,),
        in_specs=[
            pl.BlockSpec((tm, D_MODEL), row),
            y_spec, y_spec, y_spec,
            pl.BlockSpec((tm, D_MODEL), lambda i: (t0 + i, 0)),
            pl.BlockSpec((tm, D_MODEL), lambda i: (t0 + i, 1)),
            pl.BlockSpec((tm, D_MODEL), lambda i: (t0 + i, 2)),
            wbr_spec, wbr_spec, wbr_spec,
            pl.BlockSpec((D_MODEL, D_MODEL), full),
        ],
        out_specs=pl.BlockSpec((tm, D_MODEL), lambda i: (i, 0)),
        out_shape=jax.ShapeDtypeStruct((n_rows, D_MODEL), F32),
        compiler_params=_cparams(("parallel",)),
        name="merge",
    )(x2d, ya, yb, yc, proj2d, proj2d, proj2d, wa, wb, wc, wo)


def _split3(x):
    p1 = x.astype(BF16)
    r1 = x - p1.astype(F32)
    p2 = r1.astype(BF16)
    p3 = (r1 - p2.astype(F32)).astype(BF16)
    return p1, p2, p3


def _softplus(x):
    return jnp.maximum(x, 0.0) + jnp.log1p(jnp.exp(-jnp.abs(x)))


def _silu(x):
    return x * jax.nn.sigmoid(x)


def _dot(a, b):
    return jnp.dot(a, b, preferred_element_type=F32)


def _dot_nt(a, b):
    return lax.dot_general(a, b, (((1,), (1,)), ((), ())), preferred_element_type=F32)


CUM_TILE = 256
GDN_GROUP = 16


def _gdn_body(alog_ref, dtb_ref, q_ref, k_ref, v_ref, z_ref, cwq_ref, cwk_ref, cwv_ref, sm_ref, smt_ref,
              nav_ref, dtv_ref, ng_ref, o_ref,
              xp_ref, qn_ref, kn_ref, vn_ref, gcc_ref, bc_ref, gcr_ref, br_ref,
              lhs_ref, r_ref, gl_ref, of_ref, ob_ref):
    h = pl.program_id(1)
    seq = q_ref.shape[1]
    c = CHUNK_A
    n_chunks = seq // c
    dk = HEAD_W

    row_t = lax.broadcasted_iota(jnp.int32, (CUM_TILE, CUM_TILE), 0)
    col_t = lax.broadcasted_iota(jnp.int32, (CUM_TILE, CUM_TILE), 1)
    same_chunk = (row_t // c) == (col_t // c)
    tri = (jnp.where(same_chunk & (col_t <= row_t), 1.0, 0.0).astype(BF16),
           jnp.where(same_chunk & (col_t >= row_t), 1.0, 0.0).astype(BF16))
    lane_row = lax.broadcasted_iota(jnp.int32, (1, LANES), 1)
    sel_src = lax.broadcasted_iota(jnp.int32, (LANES, 4 * LANES), 0)
    sel_blk = lax.broadcasted_iota(jnp.int32, (LANES, 4 * LANES), 1) // LANES
    sel_want = jnp.where(sel_blk < 2, 2 * N_HEADS + N_HEADS * sel_blk, N_HEADS * (sel_blk - 2)) + h
    sel4 = jnp.where(sel_src == sel_want, 1.0, 0.0).astype(BF16)

    def gates_tile(t, carry):
        rows = pl.ds(pl.multiple_of(t * CUM_TILE, CUM_TILE), CUM_TILE)
        sm = sm_ref[0, rows, :]
        gates = jnp.where(lane_row < 2 * N_HEADS, jax.nn.sigmoid(sm), nav_ref[...] * _softplus(sm + dtv_ref[...]))
        rep = _dot(jnp.concatenate(_split3(gates), axis=0), sel4)
        piece = lambda p, blk: rep[p * CUM_TILE:(p + 1) * CUM_TILE, blk * LANES:(blk + 1) * LANES]
        for d in range(2):
            cs = _dot(tri[d], jnp.concatenate([piece(p, d) for p in range(3)], axis=1).astype(BF16))
            gcc_ref[d, rows, :] = cs[:, :LANES] + cs[:, LANES:2 * LANES] + cs[:, 2 * LANES:]
            bc_ref[d, rows, :] = piece(0, 2 + d) + piece(1, 2 + d) + piece(2, 2 + d)
        return carry

    lax.fori_loop(0, seq // CUM_TILE, gates_tile, 0, unroll=4)

    rc = lax.broadcasted_iota(jnp.int32, (c, c), 0)
    cc = lax.broadcasted_iota(jnp.int32, (c, c), 1)
    for d in range(2):
        a_neg = -jnp.exp(alog_ref[d, h])
        g = a_neg * _softplus(smt_ref[0, 8 + 4 * d + h] + dtb_ref[d, h])
        cum = jnp.where((rc <= cc) if d == 0 else (rc >= cc), 1.0, 0.0).astype(BF16)
        acc = jnp.zeros((n_chunks, c), F32)
        for part in _split3(g):
            acc = acc + _dot(part, cum)
        gcr_ref[d] = acc
        br_ref[d] = jax.nn.sigmoid(smt_ref[0, 4 * d + h])

    pad = 8
    zeros_pad = jnp.zeros((pad, LANES), F32)

    def conv_into(src_ref, cw_ref, dst_ref, normalise, scale):
        xp_ref[pl.ds(0, pad), :] = zeros_pad
        xp_ref[pl.ds(pad + seq, pad), :] = zeros_pad
        xp_ref[pl.ds(pad, seq), :] = src_ref[0].astype(F32)
        cw = cw_ref[...]

        def chunk(n, carry):
            base = pl.multiple_of(n * c, c)
            y = jnp.zeros((c, LANES), F32)
            for j in range(CONV_K):
                off = pad - (CONV_K - 1) // 2 + j
                y = y + xp_ref[pl.ds(base + off, c), :] * cw[j:j + 1, :]
            y = _silu(y)
            if normalise:
                y = y * lax.rsqrt(jnp.sum(y * y, axis=-1, keepdims=True) + EPS)
            dst_ref[pl.ds(base, c), :] = y * scale if scale != 1.0 else y
            return carry

        lax.fori_loop(0, n_chunks, chunk, 0, unroll=8)

    conv_into(q_ref, cwq_ref, qn_ref, True, dk ** -0.5)
    conv_into(k_ref, cwk_ref, kn_ref, True, 1.0)
    conv_into(v_ref, cwv_ref, vn_ref, False, 1.0)

    eye = jnp.where(rc == cc, 1.0, 0.0)
    masks = (((rc >= cc), (rc > cc)), ((rc <= cc), (rc < cc)))

    def prep_group(i, carry):
        chains = []
        for g in range(GDN_GROUP):
            n = i * GDN_GROUP + g
            rows = pl.ds(pl.multiple_of(n * c, c), c)
            q = qn_ref[rows, :]
            k = kn_ref[rows, :]
            k16 = k.astype(BF16)
            v16 = vn_ref[rows, :].astype(BF16)
            kq = _dot_nt(jnp.concatenate([k16, q.astype(BF16)], axis=0), k16)
            kk, qk = kq[:c, :], kq[c:, :]
            for d in range(2):
                incl, strict = masks[d]
                gcol = gcc_ref[d, rows, :]
                grow = gcr_ref[d, pl.ds(n, 1), :]
                decay = jnp.where(incl, jnp.exp(jnp.where(incl, gcol[:, :c] - grow, 0.0)), 0.0)
                x = -jnp.where(strict, kk * decay * bc_ref[d, rows, :][:, :c], 0.0)
                last = (c - 1) if d == 0 else 0
                g_last = gcol[last:last + 1, :]
                gl_ref[d, pl.ds(n, 1), :] = jnp.exp(g_last)
                chains.append(dict(d=d, n=n, rows=rows, x=x, grow=grow, k16=k16, v16=v16,
                                   a16=(qk * decay).astype(BF16), qd=q * jnp.exp(gcol),
                                   kdt16=(k * jnp.exp(g_last - gcol)).T.astype(BF16)))
        t_inv = [eye + ch["x"] for ch in chains]
        pw16 = [ch["x"].astype(BF16) for ch in chains]
        pw16 = [_dot(p, p).astype(BF16) for p in pw16]
        for _ in range(4):
            both = [_dot(jnp.concatenate([t.astype(BF16), p], axis=0), p) for t, p in zip(t_inv, pw16)]
            t_inv = [t + r[:c, :] for t, r in zip(t_inv, both)]
            pw16 = [r[c:, :].astype(BF16) for r in both]
        t_inv = [t + _dot(t.astype(BF16), p) for t, p in zip(t_inv, pw16)]
        wu = []
        for ch, t in zip(chains, t_inv):
            tb = t * br_ref[ch["d"], pl.ds(ch["n"], 1), :]
            res = _dot(jnp.concatenate([(tb * jnp.exp(ch["grow"])).astype(BF16), tb.astype(BF16)], axis=0),
                       jnp.concatenate([ch["k16"], ch["v16"]], axis=1))
            wu.append(jnp.concatenate([res[:c, :dk], res[c:, dk:]], axis=1).astype(BF16))
        for ch, wu16 in zip(chains, wu):
            d, n = ch["d"], ch["n"]
            res = _dot(jnp.concatenate([ch["kdt16"], ch["a16"]], axis=0), wu16)
            lhs_ref[d, n, pl.ds(0, dk), :] = res[:dk, :dk].astype(BF16)
            lhs_ref[d, n, pl.ds(dk, c), :] = (ch["qd"] - res[dk:, :dk]).astype(BF16)
            r_ref[d, n] = res[:dk, dk:].astype(BF16)
            (of_ref if d == 0 else ob_ref)[ch["rows"], :] = res[dk:, dk:]
        return carry

    lax.fori_loop(0, n_chunks // GDN_GROUP, prep_group, 0)

    def scan_step(i, carry):
        outs = []
        for d, s, o_out in ((0, carry[0], of_ref), (1, carry[1], ob_ref)):
            n = i if d == 0 else n_chunks - 1 - i
            rows = pl.ds(pl.multiple_of(n * c, c), c)
            res = _dot(lhs_ref[d, n], s.astype(BF16))
            o_out[rows, :] = o_out[rows, :] + res[dk:, :]
            outs.append(s * gl_ref[d, pl.ds(n, 1), :] + r_ref[d, n].astype(F32) - res[:dk, :])
        return tuple(outs)

    s0 = jnp.zeros((dk, HEAD_W), F32)
    lax.fori_loop(0, n_chunks, scan_step, (s0, s0), unroll=2)

    def finish(t, carry):
        rows = pl.ds(pl.multiple_of(t * CUM_TILE, CUM_TILE), CUM_TILE)
        o = of_ref[rows, :] + ob_ref[rows, :]
        y = o * lax.rsqrt(jnp.mean(o * o, axis=-1, keepdims=True) + EPS) * ng_ref[...]
        o_ref[0, rows, :] = (y * _silu(z_ref[0, rows, :].astype(F32))).astype(BF16)
        return carry

    lax.fori_loop(0, seq // CUM_TILE, finish, 0, unroll=2)


def _mixer_a(proj, small, small_t, conv_w, a_log, dt_bias, neg_a_vec, dtb_vec, norm_g):
    b, s, _ = proj.shape
    n_chunks = s // CHUNK_A
    assert s % CUM_TILE == 0 and n_chunks % GDN_GROUP == 0
    col = lambda cb: pl.BlockSpec((1, s, LANES), lambda bi, h: (bi, 0, cb + h))
    cw = lambda cb: pl.BlockSpec((CONV_K, LANES), lambda bi, h: (0, cb + h))
    vec = pl.BlockSpec((1, LANES), lambda bi, h: (0, 0))
    smem = pl.BlockSpec(memory_space=pltpu.SMEM)
    return pl.pallas_call(
        _gdn_body,
        grid=(b, N_HEADS),
        in_specs=[
            smem, smem,
            col(CB_AQ), col(CB_AK), col(CB_AV), col(CB_AZ),
            cw(0), cw(4), cw(8),
            pl.BlockSpec((1, s, LANES), lambda bi, h: (bi, 0, 0)),
            pl.BlockSpec((1, N_SMALL, n_chunks, CHUNK_A), lambda bi, h: (bi, 0, 0, 0)),
            vec, vec, vec,
        ],
        out_specs=pl.BlockSpec((1, s, LANES), lambda bi, h: (bi, 0, h)),
        out_shape=jax.ShapeDtypeStruct((b, s, N_HEADS * HEAD_W), BF16),
        scratch_shapes=[
            pltpu.VMEM((s + 16, LANES), F32),
            pltpu.VMEM((s, LANES), F32),
            pltpu.VMEM((s, LANES), F32),
            pltpu.VMEM((s, LANES), F32),
            pltpu.VMEM((2, s, LANES), F32),
            pltpu.VMEM((2, s, LANES), F32),
            pltpu.VMEM((2, n_chunks, CHUNK_A), F32),
            pltpu.VMEM((2, n_chunks, CHUNK_A), F32),
            pltpu.VMEM((2, n_chunks, HEAD_W + CHUNK_A, LANES), BF16),
            pltpu.VMEM((2, n_chunks, HEAD_W, LANES), BF16),
            pltpu.VMEM((2, n_chunks, LANES), F32),
            pltpu.VMEM((s, LANES), F32),
            pltpu.VMEM((s, LANES), F32),
        ],
        compiler_params=_cparams(("parallel", "arbitrary")),
        name="gdn",
    )(a_log, dt_bias, proj, proj, proj, proj, conv_w, conv_w, conv_w, small, small_t, neg_a_vec, dtb_vec, norm_g)


HGRN_GROUP = 16


def _hgrn_body(q_ref, i_ref, ff_ref, fb_ref, z_ref, lb_ref, ng_ref, o_ref,
               qs_ref, kk_ref, gc_ref, gt_ref, qd_ref, kd_ref, oi_ref, of_ref, ob_ref):
    seq = q_ref.shape[1]
    c = CHUNK_B
    n_chunks = seq // c
    dk = HEAD_W

    row_t = lax.broadcasted_iota(jnp.int32, (CUM_TILE, CUM_TILE), 0)
    col_t = lax.broadcasted_iota(jnp.int32, (CUM_TILE, CUM_TILE), 1)
    same_chunk = (row_t // c) == (col_t // c)
    tri = (jnp.where(same_chunk & (col_t <= row_t), 1.0, 0.0).astype(BF16),
           jnp.where(same_chunk & (col_t >= row_t), 1.0, 0.0).astype(BF16))

    def gates_tile(t, carry):
        rows = pl.ds(pl.multiple_of(t * CUM_TILE, CUM_TILE), CUM_TILE)
        qs = _silu(q_ref[0, rows, :].astype(F32)) * (dk ** -0.5)
        qs_ref[rows, :] = qs
        for d, f_ref in ((0, ff_ref), (1, fb_ref)):
            bf = f_ref[0, rows, :].astype(F32)
            kk = (1.0 - lb_ref[d:d + 1, :]) * jax.nn.sigmoid(-bf)
            log_f = jnp.log1p(-jnp.minimum(kk, 1.0 - F_MIN_GAP))
            gc = jnp.zeros((CUM_TILE, LANES), F32)
            for part in _split3(log_f):
                gc = gc + _dot(tri[d], part)
            end = (c - 1) if d == 0 else 0
            gt = jnp.concatenate([jnp.broadcast_to(gc[k * c + end:k * c + end + 1, :], (c, LANES))
                                  for k in range(CUM_TILE // c)], axis=0)
            kk_ref[d, rows, :] = kk
            gc_ref[d, rows, :] = gc * LOG2E
            gt_ref[d, rows, :] = gt
            qd_ref[d, rows, :] = (qs * jnp.exp(gc)).astype(BF16)
            kd_ref[d, rows, :] = (kk * jnp.exp(gt - gc)).astype(BF16)
        return carry

    lax.fori_loop(0, seq // CUM_TILE, gates_tile, 0, unroll=4)

    jrow = lax.broadcasted_iota(jnp.int32, (SUB, LANES), 0)
    ones_w = jnp.ones((LANES, LANES), BF16)
    n_pieces = c + c // 2
    piece = lax.broadcasted_iota(jnp.int32, (c, n_pieces * SUB), 1) // SUB
    out_row = lax.broadcasted_iota(jnp.int32, (c, n_pieces * SUB), 0)
    row_of_piece = (jnp.where(piece < SUB, piece, SUB + (piece - SUB) // 2),
                    jnp.where(piece < c, piece // 2, piece - SUB))
    sel_both = jnp.concatenate([jnp.where(rp == out_row, 1.0, 0.0) for rp in row_of_piece], axis=1).astype(BF16)

    def scan_step(i, carry):
        tiles, vstacks, row_sets = [], [], []
        for g in range(HGRN_GROUP):
            rows = pl.ds(pl.multiple_of((i * HGRN_GROUP + g) * c, c), c)
            qs = qs_ref[rows, :]
            v = i_ref[0, rows, :].astype(F32)
            parts, vparts = [], []
            for d in range(2):
                kk = kk_ref[d, rows, :]
                gc2 = gc_ref[d, rows, :]
                for r in range(c):
                    for j0 in (0, SUB):
                        if (j0 > r) if d == 0 else (j0 + SUB - 1 < r):
                            continue
                        pair = jnp.exp2(gc2[r:r + 1, :] - gc2[j0:j0 + SUB, :])
                        if not ((j0 + SUB - 1 <= r) if d == 0 else (j0 >= r)):
                            mask = (jrow + j0 <= r) if d == 0 else (jrow + j0 >= r)
                            pair = jnp.where(mask, pair, 0.0)
                        parts.append(pair * kk[j0:j0 + SUB, :] * qs[r:r + 1, :])
                        vparts.append(v[j0:j0 + SUB, :])
            tiles.append(jnp.concatenate(parts, axis=0).astype(BF16))
            vstacks.append(jnp.concatenate(vparts, axis=0))
            row_sets.append(rows)
        a_reps = [_dot(t, ones_w) for t in tiles]
        upds = {}
        for g in range(HGRN_GROUP):
            for d in range(2):
                n = i * HGRN_GROUP + g
                n = n if d == 0 else n_chunks - 1 - n
                rows = pl.ds(pl.multiple_of(n * c, c), c)
                upds[(g, d)] = (rows, lax.dot_general(i_ref[0, rows, :], kd_ref[d, rows, :], (((0,), (0,)), ((), ())),
                                                      preferred_element_type=F32))
        for rows, a_rep, v_stack in zip(row_sets, a_reps, vstacks):
            oi_ref[rows, :] = _dot(sel_both, (a_rep * v_stack).astype(BF16))
        st = list(carry)
        for g in range(HGRN_GROUP):
            for d in range(2):
                rows, upd = upds[(g, d)]
                (of_ref if d == 0 else ob_ref)[rows, :] = _dot_nt(qd_ref[d, rows, :], st[d].astype(BF16))
                e_row = jnp.exp(gt_ref[d, pl.ds(rows.start, 1), :])
                st[d] = st[d] * e_row + upd
        return tuple(st)

    s0 = jnp.zeros((HEAD_W, dk), F32)
    lax.fori_loop(0, n_chunks // HGRN_GROUP, scan_step, (s0, s0))

    def finish(t, carry):
        rows = pl.ds(pl.multiple_of(t * CUM_TILE, CUM_TILE), CUM_TILE)
        o = oi_ref[rows, :] + of_ref[rows, :] + ob_ref[rows, :]
        y = o * lax.rsqrt(jnp.mean(o * o, axis=-1, keepdims=True) + EPS) * ng_ref[...]
        o_ref[0, rows, :] = (y * _silu(z_ref[0, rows, :].astype(F32))).astype(BF16)
        return carry

    lax.fori_loop(0, seq // CUM_TILE, finish, 0, unroll=2)


def _mixer_b(proj, lb, norm_g):
    b, s, _ = proj.shape
    assert s % CUM_TILE == 0 and (s // CHUNK_B) % HGRN_GROUP == 0
    col = lambda cb: pl.BlockSpec((1, s, LANES), lambda bi, h: (bi, 0, cb + h))
    return pl.pallas_call(
        _hgrn_body,
        grid=(b, N_HEADS),
        in_specs=[
            col(CB_BQ), col(CB_BI), col(CB_BF), col(CB_BF + N_HEADS), col(CB_BZ),
            pl.BlockSpec((2, LANES), lambda bi, h: (0, h)),
            pl.BlockSpec((1, LANES), lambda bi, h: (0, 0)),
        ],
        out_specs=pl.BlockSpec((1, s, LANES), lambda bi, h: (bi, 0, h)),
        out_shape=jax.ShapeDtypeStruct((b, s, N_HEADS * HEAD_W), BF16),
        scratch_shapes=[
            pltpu.VMEM((s, LANES), F32),
            pltpu.VMEM((2, s, LANES), F32),
            pltpu.VMEM((2, s, LANES), F32),
            pltpu.VMEM((2, s, LANES), F32),
            pltpu.VMEM((2, s, LANES), BF16),
            pltpu.VMEM((2, s, LANES), BF16),
            pltpu.VMEM((s, LANES), F32),
            pltpu.VMEM((s, LANES), F32),
            pltpu.VMEM((s, LANES), F32),
        ],
        compiler_params=_cparams(("parallel", "arbitrary")),
        name="hgrn",
    )(proj, proj, proj, proj, proj, lb, norm_g)


def _rope_tables(seq):
    half = ROPE_DIM // 2
    inv = 1.0 / (ROPE_THETA ** (jnp.arange(0, ROPE_DIM, 2, dtype=F32) / ROPE_DIM))
    ang = jnp.arange(seq, dtype=F32)[:, None] * inv[None, :]
    cos, sin = jnp.cos(ang), jnp.sin(ang)
    one = jnp.ones((seq, D_C - ROPE_DIM), F32)
    zero = jnp.zeros((seq, D_C - ROPE_DIM), F32)
    zh = jnp.zeros((seq, half), F32)
    c_map = jnp.concatenate([cos, cos, one], axis=1)
    s1_map = jnp.concatenate([zh, sin, zero], axis=1)
    s2_map = jnp.concatenate([-sin, zh, zero], axis=1)
    tile2 = lambda t: jnp.concatenate([t, t], axis=1)
    return tile2(c_map), tile2(s1_map), tile2(s2_map)


def _lower_bounds(lb_logits):
    p = jax.nn.softmax(lb_logits.astype(F32), axis=1)
    return jnp.cumsum(p, axis=1) - p[:, :1]


def kernel(x_prompt, x_sample, norm_g, w_in, conv_w, a_log, dt_bias, gdn_norm_g, hgrn_lb_logits, hgrn_norm_g,
           q_norm_g, k_norm_g, diff_lambda, subln_g, w_br_a, w_br_b, w_br_c, w_out):
    nb_p = x_prompt.shape[0]
    x = jnp.concatenate([x_prompt, x_sample], axis=0)
    b, s, d = x.shape
    depth = w_in.shape[0]
    n_in = w_in.shape[-1]
    gate0 = n_in - 3 * D_MODEL
    w_main = jnp.concatenate([w_in[:, :, gate0:], w_in[:, :, :2048], w_in[:, :, 2048 + N_SMALL:gate0]],
                             axis=-1).astype(BF16)
    w_small = jnp.pad(w_in[:, :, 2048:2048 + N_SMALL], ((0, 0), (0, 0), (0, LANES - N_SMALL))).astype(BF16)
    wa, wb, wc, wo = (w.astype(BF16) for w in (w_br_a, w_br_b, w_br_c, w_out))
    cos_t, s1_t, s2_t = _rope_tables(s)
    lbs = _lower_bounds(hgrn_lb_logits)
    lane_pad = ((0, 0), (2 * N_HEADS, LANES - 4 * N_HEADS))
    neg_a_vec = jnp.pad(-jnp.exp(a_log.astype(F32)).reshape(depth, 2 * N_HEADS), lane_pad)[:, None, :]
    dtb_vec = jnp.pad(dt_bias.astype(F32).reshape(depth, 2 * N_HEADS), lane_pad)[:, None, :]
    lp = diff_lambda.astype(F32)
    lam_dyn = jnp.exp(jnp.sum(lp[:, 0] * lp[:, 1], axis=-1)) - jnp.exp(jnp.sum(lp[:, 2] * lp[:, 3], axis=-1))

    x2d = x.reshape(b * s, d)
    for l in range(depth):
        lambda_init = 0.8 - 0.6 * math.exp(-0.3 * l)
        proj2d, small2d = _inproj(x2d, norm_g[l][None, :], w_main[l], w_small[l])
        proj = proj2d.reshape(b, s, N_MAIN)
        small = small2d.reshape(b, s, LANES)
        small_t = small[:, :, :N_SMALL].transpose(0, 2, 1).reshape(b, N_SMALL, s // CHUNK_A, CHUNK_A)
        ya = _mixer_a(proj, small, small_t, conv_w[l], a_log[l], dt_bias[l], neg_a_vec[l], dtb_vec[l],
                      gdn_norm_g[l][None, :])
        yb = _mixer_b(proj, lbs[:, l], hgrn_norm_g[l][None, :])
        qn, kn = _qkprep(proj, cos_t, s1_t, s2_t, q_norm_g[l].reshape(1, LANES), k_norm_g[l].reshape(1, LANES))
        scal = jnp.stack([lam_dyn[l] + lambda_init, jnp.asarray(1.0 - lambda_init, F32)]).astype(F32)
        yc = _attn(scal, qn, kn, proj, subln_g[l][None, :])
        merge_args = (x2d, ya.reshape(b * s, -1), yb.reshape(b * s, -1), yc.reshape(b * s, -1), proj2d,
                      wa[l], wb[l], wc[l], wo[l])
        if l + 1 < depth:
            x2d = _merge(*merge_args)
    y_prompt = _merge(*merge_args, row0=0, n_rows=nb_p * s)
    y_sample = _merge(*merge_args, row0=nb_p * s, n_rows=(b - nb_p) * s)
    return (y_prompt.reshape(nb_p, s, d), y_sample.reshape(b - nb_p, s, d))
```

```python
import functools
import math

import jax
import jax.numpy as jnp
from jax import lax
from jax.experimental import pallas as pl
from jax.experimental.pallas import tpu as pltpu

F32 = jnp.float32
BF16 = jnp.bfloat16
EPS = 1e-6
F_MIN_GAP = 1e-6
LOG2E = math.log2(math.e)

D_MODEL = 1024
LANES = 128
SUB = 8
N_HEADS = 4
HEAD_W = 128
CONV_K = 5
CHUNK_A = 64
CHUNK_B = 16
D_C = 64
ROPE_DIM = D_C // 4
ROPE_THETA = 500000.0
N_SMALL = 16
N_MAIN = 9728
CB_AQ, CB_AK, CB_AV, CB_AZ = 24, 28, 32, 36
CB_BQ, CB_BI, CB_BF, CB_BZ = 40, 44, 48, 56
CB_CQ, CB_CK, CB_CV, CB_CZ = 60, 64, 68, 72
VMEM_LIMIT = 56 * 1024 * 1024


def _cparams(sem):
    return pltpu.CompilerParams(dimension_semantics=sem, vmem_limit_bytes=VMEM_LIMIT)


def _inproj_body(x_ref, g_ref, w_ref, ws_ref, o_ref, os_ref, h_ref):
    @pl.when(pl.program_id(1) == 0)
    def _():
        x = x_ref[...]
        h = x * lax.rsqrt(jnp.mean(x * x, axis=-1, keepdims=True) + EPS) * g_ref[...]
        hb = h.astype(BF16)
        h_ref[...] = hb
        os_ref[...] = jnp.dot(hb, ws_ref[...], preferred_element_type=F32)

    o_ref[...] = jnp.dot(h_ref[...], w_ref[...], preferred_element_type=F32).astype(BF16)


def _inproj(x2d, g, w_main, w_small, tm=1024, tn=2432):
    t = x2d.shape[0]
    while t % tm:
        tm //= 2
    return pl.pallas_call(
        _inproj_body,
        grid=(t // tm, N_MAIN // tn),
        in_specs=[
            pl.BlockSpec((tm, D_MODEL), lambda i, j: (i, 0)),
            pl.BlockSpec((1, D_MODEL), lambda i, j: (0, 0)),
            pl.BlockSpec((D_MODEL, tn), lambda i, j: (0, j)),
            pl.BlockSpec((D_MODEL, LANES), lambda i, j: (0, 0)),
        ],
        out_specs=[
            pl.BlockSpec((tm, tn), lambda i, j: (i, j)),
            pl.BlockSpec((tm, LANES), lambda i, j: (i, 0)),
        ],
        out_shape=[
            jax.ShapeDtypeStruct((t, N_MAIN), BF16),
            jax.ShapeDtypeStruct((t, LANES), F32),
        ],
        scratch_shapes=[pltpu.VMEM((tm, D_MODEL), BF16)],
        compiler_params=_cparams(("parallel", "arbitrary")),
        name="inproj",
    )(x2d, g, w_main, w_small)


def _qkprep_body(q_ref, k_ref, c_ref, s1_ref, s2_ref, qg_ref, kg_ref, qo_ref, ko_ref, *, q_scale):
    cos, s1, s2 = c_ref[...], s1_ref[...], s2_ref[...]
    same_map = jnp.where(lax.broadcasted_iota(jnp.int32, (LANES, LANES), 0) // D_C
                         == lax.broadcasted_iota(jnp.int32, (LANES, LANES), 1) // D_C, 1.0, 0.0).astype(BF16)

    def prep(x, g):
        x = x.astype(F32)
        x2 = x * x
        x2_hi = x2.astype(BF16)
        x2_lo = (x2 - x2_hi.astype(F32)).astype(BF16)
        ms = (_dot(x2_hi, same_map) + _dot(x2_lo, same_map)) * (1.0 / D_C)
        y = x * lax.rsqrt(ms + EPS) * g
        return y * cos + pltpu.roll(y, ROPE_DIM // 2, 1) * s1 + pltpu.roll(y, LANES - ROPE_DIM // 2, 1) * s2

    qo_ref[0] = (prep(q_ref[0], qg_ref[...]) * q_scale).astype(BF16)
    ko_ref[0] = prep(k_ref[0], kg_ref[...]).astype(BF16)


def _qkprep(proj, cos_t, s1_t, s2_t, qg, kg, ts=1024):
    b, s, _ = proj.shape
    ts = min(ts, s)
    q_scale = (D_C ** -0.5) * math.log2(math.e)
    tab = pl.BlockSpec((ts, LANES), lambda bi, si, h: (si, 0))
    vec = pl.BlockSpec((1, LANES), lambda bi, si, h: (0, 0))
    out = pl.BlockSpec((1, ts, LANES), lambda bi, si, h: (bi, si, h))
    return pl.pallas_call(
        functools.partial(_qkprep_body, q_scale=q_scale),
        grid=(b, s // ts, N_HEADS),
        in_specs=[
            pl.BlockSpec((1, ts, LANES), lambda bi, si, h: (bi, si, CB_CQ + h)),
            pl.BlockSpec((1, ts, LANES), lambda bi, si, h: (bi, si, CB_CK + h)),
            tab, tab, tab, vec, vec,
        ],
        out_specs=[out, out],
        out_shape=[jax.ShapeDtypeStruct((b, s, N_HEADS * HEAD_W), BF16)] * 2,
        compiler_params=_cparams(("parallel", "parallel", "parallel")),
        name="qkprep",
    )(proj, proj, cos_t, s1_t, s2_t, qg, kg)


ATTN_SUB = 256
ATTN_AHEAD = 1


def _attn_body(sc_ref, q_ref, k_ref, v_ref, z_ref, g_ref, o_ref):
    lam = sc_ref[0]
    out_scale = sc_ref[1]
    k = k_ref[0]
    v = v_ref[0]
    lane = lax.broadcasted_iota(jnp.int32, (1, LANES), 1)
    n_sub = q_ref.shape[1] // ATTN_SUB

    def scores(r, m):
        q = q_ref[0, r * ATTN_SUB:(r + 1) * ATTN_SUB, :]
        qm = jnp.where((lane < D_C) if m == 0 else (lane >= D_C), q, jnp.zeros_like(q))
        return _dot_nt(qm, k)

    def weighted(s):
        p = jnp.exp2(s - jnp.max(s, axis=-1, keepdims=True))
        return _dot(p.astype(BF16), v) / jnp.sum(p, axis=-1, keepdims=True)

    units = [(r, m) for r in range(n_sub) for m in range(2)]
    outs = {}
    pending = [scores(*u) for u in units[:ATTN_AHEAD]]
    for idx, unit in enumerate(units):
        if idx + ATTN_AHEAD < len(units):
            pending.append(scores(*units[idx + ATTN_AHEAD]))
        outs[unit] = weighted(pending.pop(0))
    for r in range(n_sub):
        rows = slice(r * ATTN_SUB, (r + 1) * ATTN_SUB)
        o = outs[(r, 0)] - lam * outs[(r, 1)]
        y = o * lax.rsqrt(jnp.mean(o * o, axis=-1, keepdims=True) + EPS) * g_ref[...]
        z = z_ref[0, rows, :].astype(F32)
        o_ref[0, rows, :] = (y * out_scale * (z * jax.nn.sigmoid(z))).astype(BF16)


def _attn(scal, qn, kn, proj, subln_g, tq=2048):
    b, s, _ = proj.shape
    tq = min(tq, s)
    assert tq % ATTN_SUB == 0 and s % tq == 0
    return pl.pallas_call(
        _attn_body,
        grid=(b, N_HEADS, s // tq),
        in_specs=[
            pl.BlockSpec(memory_space=pltpu.SMEM),
            pl.BlockSpec((1, tq, LANES), lambda bi, h, qi: (bi, qi, h)),
            pl.BlockSpec((1, s, LANES), lambda bi, h, qi: (bi, 0, h)),
            pl.BlockSpec((1, s, LANES), lambda bi, h, qi: (bi, 0, CB_CV + h)),
            pl.BlockSpec((1, tq, LANES), lambda bi, h, qi: (bi, qi, CB_CZ + h)),
            pl.BlockSpec((1, LANES), lambda bi, h, qi: (0, 0)),
        ],
        out_specs=pl.BlockSpec((1, tq, LANES), lambda bi, h, qi: (bi, qi, h)),
        out_shape=jax.ShapeDtypeStruct((b, s, N_HEADS * HEAD_W), BF16),
        compiler_params=_cparams(("parallel", "parallel", "arbitrary")),
        name="diffattn",
    )(scal, qn, kn, proj, proj, subln_g)


def _merge_body(x_ref, ya_ref, yb_ref, yc_ref, g0_ref, g1_ref, g2_ref, wa_ref, wb_ref, wc_ref, wo_ref, o_ref):
    def branch(y_ref, w_ref, g_ref):
        return jax.nn.sigmoid(g_ref[...].astype(F32)) * jnp.dot(y_ref[...], w_ref[...], preferred_element_type=F32)

    merged = branch(ya_ref, wa_ref, g0_ref) + branch(yb_ref, wb_ref, g1_ref) + branch(yc_ref, wc_ref, g2_ref)
    o_ref[...] = x_ref[...] + jnp.dot(merged.astype(BF16), wo_ref[...], preferred_element_type=F32)


def _merge(x2d, ya, yb, yc, proj2d, wa, wb, wc, wo, row0=0, n_rows=None, tm=512):
    n_rows = x2d.shape[0] if n_rows is None else n_rows
    assert row0 % tm == 0 and n_rows % tm == 0
    t0 = row0 // tm
    w_br = N_HEADS * HEAD_W
    row = lambda i: (t0 + i, 0)
    full = lambda i: (0, 0)
    y_spec = pl.BlockSpec((tm, w_br), row)
    wbr_spec = pl.BlockSpec((w_br, D_MODEL), full)
    return pl.pallas_call(
        _merge_body,
        grid=(n_rows // tm,),
        in_specs=[
            pl.BlockSpec((tm, D_MODEL), row),
            y_spec, y_spec, y_spec,
            pl.BlockSpec((tm, D_MODEL), lambda i: (t0 + i, 0)),
            pl.BlockSpec((tm, D_MODEL), lambda i: (t0 + i, 1)),
            pl.BlockSpec((tm, D_MODEL), lambda i: (t0 + i, 2)),
            wbr_spec, wbr_spec, wbr_spec,
            pl.BlockSpec((D_MODEL, D_MODEL), full),
        ],
        out_specs=pl.BlockSpec((tm, D_MODEL), lambda i: (i, 0)),
        out_shape=jax.ShapeDtypeStruct((n_rows, D_MODEL), F32),
        compiler_params=_cparams(("parallel",)),
        name="merge",
    )(x2d, ya, yb, yc, proj2d, proj2d, proj2d, wa, wb, wc, wo)


def _split3(x):
    p1 = x.astype(BF16)
    r1 = x - p1.astype(F32)
    p2 = r1.astype(BF16)
    p3 = (r1 - p2.astype(F32)).astype(BF16)
    return p1, p2, p3


def _softplus(x):
    return jnp.maximum(x, 0.0) + jnp.log1p(jnp.exp(-jnp.abs(x)))


def _silu(x):
    return x * jax.nn.sigmoid(x)


def _dot(a, b):
    return jnp.dot(a, b, preferred_element_type=F32)


def _dot_nt(a, b):
    return lax.dot_general(a, b, (((1,), (1,)), ((), ())), preferred_element_type=F32)


CUM_TILE = 256
GDN_GROUP = 16


def _gdn_body(alog_ref, dtb_ref, q_ref, k_ref, v_ref, z_ref, cwq_ref, cwk_ref, cwv_ref, sm_ref, smt_ref,
              nav_ref, dtv_ref, ng_ref, o_ref,
              xp_ref, qn_ref, kn_ref, vn_ref, gcc_ref, bc_ref, gcr_ref, br_ref,
              lhs_ref, r_ref, gl_ref, of_ref, ob_ref):
    h = pl.program_id(1)
    seq = q_ref.shape[1]
    c = CHUNK_A
    n_chunks = seq // c
    dk = HEAD_W

    row_t = lax.broadcasted_iota(jnp.int32, (CUM_TILE, CUM_TILE), 0)
    col_t = lax.broadcasted_iota(jnp.int32, (CUM_TILE, CUM_TILE), 1)
    same_chunk = (row_t // c) == (col_t // c)
    tri = (jnp.where(same_chunk & (col_t <= row_t), 1.0, 0.0).astype(BF16),
           jnp.where(same_chunk & (col_t >= row_t), 1.0, 0.0).astype(BF16))
    lane_row = lax.broadcasted_iota(jnp.int32, (1, LANES), 1)
    sel_src = lax.broadcasted_iota(jnp.int32, (LANES, 4 * LANES), 0)
    sel_blk = lax.broadcasted_iota(jnp.int32, (LANES, 4 * LANES), 1) // LANES
    sel_want = jnp.where(sel_blk < 2, 2 * N_HEADS + N_HEADS * sel_blk, N_HEADS * (sel_blk - 2)) + h
    sel4 = jnp.where(sel_src == sel_want, 1.0, 0.0).astype(BF16)

    def gates_tile(t, carry):
        rows = pl.ds(pl.multiple_of(t * CUM_TILE, CUM_TILE), CUM_TILE)
        sm = sm_ref[0, rows, :]
        gates = jnp.where(lane_row < 2 * N_HEADS, jax.nn.sigmoid(sm), nav_ref[...] * _softplus(sm + dtv_ref[...]))
        rep = _dot(jnp.concatenate(_split3(gates), axis=0), sel4)
        piece = lambda p, blk: rep[p * CUM_TILE:(p + 1) * CUM_TILE, blk * LANES:(blk + 1) * LANES]
        for d in range(2):
            cs = _dot(tri[d], jnp.concatenate([piece(p, d) for p in range(3)], axis=1).astype(BF16))
            gcc_ref[d, rows, :] = cs[:, :LANES] + cs[:, LANES:2 * LANES] + cs[:, 2 * LANES:]
            bc_ref[d, rows, :] = piece(0, 2 + d) + piece(1, 2 + d) + piece(2, 2 + d)
        return carry

    lax.fori_loop(0, seq // CUM_TILE, gates_tile, 0, unroll=4)

    rc = lax.broadcasted_iota(jnp.int32, (c, c), 0)
    cc = lax.broadcasted_iota(jnp.int32, (c, c), 1)
    for d in range(2):
        a_neg = -jnp.exp(alog_ref[d, h])
        g = a_neg * _softplus(smt_ref[0, 8 + 4 * d + h] + dtb_ref[d, h])
        cum = jnp.where((rc <= cc) if d == 0 else (rc >= cc), 1.0, 0.0).astype(BF16)
        acc = jnp.zeros((n_chunks, c), F32)
        for part in _split3(g):
            acc = acc + _dot(part, cum)
        gcr_ref[d] = acc
        br_ref[d] = jax.nn.sigmoid(smt_ref[0, 4 * d + h])

    pad = 8
    zeros_pad = jnp.zeros((pad, LANES), F32)

    def conv_into(src_ref, cw_ref, dst_ref, normalise, scale):
        xp_ref[pl.ds(0, pad), :] = zeros_pad
        xp_ref[pl.ds(pad + seq, pad), :] = zeros_pad
        xp_ref[pl.ds(pad, seq), :] = src_ref[0].astype(F32)
        cw = cw_ref[...]

        def chunk(n, carry):
            base = pl.multiple_of(n * c, c)
            y = jnp.zeros((c, LANES), F32)
            for j in range(CONV_K):
                off = pad - (CONV_K - 1) // 2 + j
                y = y + xp_ref[pl.ds(base + off, c), :] * cw[j:j + 1, :]
            y = _silu(y)
            if normalise:
                y = y * lax.rsqrt(jnp.sum(y * y, axis=-1, keepdims=True) + EPS)
            dst_ref[pl.ds(base, c), :] = y * scale if scale != 1.0 else y
            return carry

        lax.fori_loop(0, n_chunks, chunk, 0, unroll=16)

    conv_into(q_ref, cwq_ref, qn_ref, True, dk ** -0.5)
    conv_into(k_ref, cwk_ref, kn_ref, True, 1.0)
    conv_into(v_ref, cwv_ref, vn_ref, False, 1.0)

    eye = jnp.where(rc == cc, 1.0, 0.0)
    masks = (((rc >= cc), (rc > cc)), ((rc <= cc), (rc < cc)))

    def prep_group(i, carry):
        chains = []
        for g in range(GDN_GROUP):
            n = i * GDN_GROUP + g
            rows = pl.ds(pl.multiple_of(n * c, c), c)
            q = qn_ref[rows, :]
            k = kn_ref[rows, :]
            k16 = k.astype(BF16)
            v16 = vn_ref[rows, :].astype(BF16)
            kq = _dot_nt(jnp.concatenate([k16, q.astype(BF16)], axis=0), k16)
            kk, qk = kq[:c, :], kq[c:, :]
            for d in range(2):
                incl, strict = masks[d]
                gcol = gcc_ref[d, rows, :]
                grow = gcr_ref[d, pl.ds(n, 1), :]
                decay = jnp.where(incl, jnp.exp(jnp.where(incl, gcol[:, :c] - grow, 0.0)), 0.0)
                x = -jnp.where(strict, kk * decay * bc_ref[d, rows, :][:, :c], 0.0)
                last = (c - 1) if d == 0 else 0
                g_last = gcol[last:last + 1, :]
                gl_ref[d, pl.ds(n, 1), :] = jnp.exp(g_last)
                chains.append(dict(d=d, n=n, rows=rows, x=x, grow=grow, k16=k16, v16=v16,
                                   a16=(qk * decay).astype(BF16), qd=q * jnp.exp(gcol),
                                   kdt16=(k * jnp.exp(g_last - gcol)).T.astype(BF16)))
        t_inv = [eye + ch["x"] for ch in chains]
        pw16 = [ch["x"].astype(BF16) for ch in chains]
        pw16 = [_dot(p, p).astype(BF16) for p in pw16]
        for _ in range(4):
            both = [_dot(jnp.concatenate([t.astype(BF16), p], axis=0), p) for t, p in zip(t_inv, pw16)]
            t_inv = [t + r[:c, :] for t, r in zip(t_inv, both)]
            pw16 = [r[c:, :].astype(BF16) for r in both]
        t_inv = [t + _dot(t.astype(BF16), p) for t, p in zip(t_inv, pw16)]
        wu = []
        for ch, t in zip(chains, t_inv):
            tb = t * br_ref[ch["d"], pl.ds(ch["n"], 1), :]
            res = _dot(jnp.concatenate([(tb * jnp.exp(ch["grow"])).astype(BF16), tb.astype(BF16)], axis=0),
                       jnp.concatenate([ch["k16"], ch["v16"]], axis=1))
            wu.append(jnp.concatenate([res[:c, :dk], res[c:, dk:]], axis=1).astype(BF16))
        for ch, wu16 in zip(chains, wu):
            d, n = ch["d"], ch["n"]
            res = _dot(jnp.concatenate([ch["kdt16"], ch["a16"]], axis=0), wu16)
            lhs_ref[d, n, pl.ds(0, dk), :] = res[:dk, :dk].astype(BF16)
            lhs_ref[d, n, pl.ds(dk, c), :] = (ch["qd"] - res[dk:, :dk]).astype(BF16)
            r_ref[d, n] = res[:dk, dk:].astype(BF16)
            (of_ref if d == 0 else ob_ref)[ch["rows"], :] = res[dk:, dk:]
        return carry

    lax.fori_loop(0, n_chunks // GDN_GROUP, prep_group, 0)

    def scan_step(i, carry):
        outs = []
        for d, s, o_out in ((0, carry[0], of_ref), (1, carry[1], ob_ref)):
            n = i if d == 0 else n_chunks - 1 - i
            rows = pl.ds(pl.multiple_of(n * c, c), c)
            res = _dot(lhs_ref[d, n], s.astype(BF16))
            o_out[rows, :] = o_out[rows, :] + res[dk:, :]
            outs.append(s * gl_ref[d, pl.ds(n, 1), :] + r_ref[d, n].astype(F32) - res[:dk, :])
        return tuple(outs)

    s0 = jnp.zeros((dk, HEAD_W), F32)
    lax.fori_loop(0, n_chunks, scan_step, (s0, s0), unroll=4)

    def finish(t, carry):
        rows = pl.ds(pl.multiple_of(t * CUM_TILE, CUM_TILE), CUM_TILE)
        o = of_ref[rows, :] + ob_ref[rows, :]
        y = o * lax.rsqrt(jnp.mean(o * o, axis=-1, keepdims=True) + EPS) * ng_ref[...]
        o_ref[0, rows, :] = (y * _silu(z_ref[0, rows, :].astype(F32))).astype(BF16)
        return carry

    lax.fori_loop(0, seq // CUM_TILE, finish, 0, unroll=2)


def _mixer_a(proj, small, small_t, conv_w, a_log, dt_bias, neg_a_vec, dtb_vec, norm_g):
    b, s, _ = proj.shape
    n_chunks = s // CHUNK_A
    assert s % CUM_TILE == 0 and n_chunks % GDN_GROUP == 0
    col = lambda cb: pl.BlockSpec((1, s, LANES), lambda bi, h: (bi, 0, cb + h))
    cw = lambda cb: pl.BlockSpec((CONV_K, LANES), lambda bi, h: (0, cb + h))
    vec = pl.BlockSpec((1, LANES), lambda bi, h: (0, 0))
    smem = pl.BlockSpec(memory_space=pltpu.SMEM)
    return pl.pallas_call(
        _gdn_body,
        grid=(b, N_HEADS),
        in_specs=[
            smem, smem,
            col(CB_AQ), col(CB_AK), col(CB_AV), col(CB_AZ),
            cw(0), cw(4), cw(8),
            pl.BlockSpec((1, s, LANES), lambda bi, h: (bi, 0, 0)),
            pl.BlockSpec((1, N_SMALL, n_chunks, CHUNK_A), lambda bi, h: (bi, 0, 0, 0)),
            vec, vec, vec,
        ],
        out_specs=pl.BlockSpec((1, s, LANES), lambda bi, h: (bi, 0, h)),
        out_shape=jax.ShapeDtypeStruct((b, s, N_HEADS * HEAD_W), BF16),
        scratch_shapes=[
            pltpu.VMEM((s + 16, LANES), F32),
            pltpu.VMEM((s, LANES), F32),
            pltpu.VMEM((s, LANES), F32),
            pltpu.VMEM((s, LANES), F32),
            pltpu.VMEM((2, s, LANES), F32),
            pltpu.VMEM((2, s, LANES), F32),
            pltpu.VMEM((2, n_chunks, CHUNK_A), F32),
            pltpu.VMEM((2, n_chunks, CHUNK_A), F32),
            pltpu.VMEM((2, n_chunks, HEAD_W + CHUNK_A, LANES), BF16),
            pltpu.VMEM((2, n_chunks, HEAD_W, LANES), BF16),
            pltpu.VMEM((2, n_chunks, LANES), F32),
            pltpu.VMEM((s, LANES), F32),
            pltpu.VMEM((s, LANES), F32),
        ],
        compiler_params=_cparams(("parallel", "arbitrary")),
        name="gdn",
    )(a_log, dt_bias, proj, proj, proj, proj, conv_w, conv_w, conv_w, small, small_t, neg_a_vec, dtb_vec, norm_g)


HGRN_GROUP = 16


def _hgrn_body(q_ref, i_ref, ff_ref, fb_ref, z_ref, lb_ref, ng_ref, o_ref,
               qs_ref, kk_ref, gc_ref, gt_ref, qd_ref, kd_ref, oi_ref, of_ref, ob_ref):
    seq = q_ref.shape[1]
    c = CHUNK_B
    n_chunks = seq // c
    dk = HEAD_W

    row_t = lax.broadcasted_iota(jnp.int32, (CUM_TILE, CUM_TILE), 0)
    col_t = lax.broadcasted_iota(jnp.int32, (CUM_TILE, CUM_TILE), 1)
    same_chunk = (row_t // c) == (col_t // c)
    tri = (jnp.where(same_chunk & (col_t <= row_t), 1.0, 0.0).astype(BF16),
           jnp.where(same_chunk & (col_t >= row_t), 1.0, 0.0).astype(BF16))

    def gates_tile(t, carry):
        rows = pl.ds(pl.multiple_of(t * CUM_TILE, CUM_TILE), CUM_TILE)
        qs = _silu(q_ref[0, rows, :].astype(F32)) * (dk ** -0.5)
        qs_ref[rows, :] = qs
        for d, f_ref in ((0, ff_ref), (1, fb_ref)):
            bf = f_ref[0, rows, :].astype(F32)
            kk = (1.0 - lb_ref[d:d + 1, :]) * jax.nn.sigmoid(-bf)
            log_f = jnp.log1p(-jnp.minimum(kk, 1.0 - F_MIN_GAP))
            gc = jnp.zeros((CUM_TILE, LANES), F32)
            for part in _split3(log_f):
                gc = gc + _dot(tri[d], part)
            end = (c - 1) if d == 0 else 0
            gt = jnp.concatenate([jnp.broadcast_to(gc[k * c + end:k * c + end + 1, :], (c, LANES))
                                  for k in range(CUM_TILE // c)], axis=0)
            kk_ref[d, rows, :] = kk
            gc_ref[d, rows, :] = gc * LOG2E
            gt_ref[d, rows, :] = gt
            qd_ref[d, rows, :] = (qs * jnp.exp(gc)).astype(BF16)
            kd_ref[d, rows, :] = (kk * jnp.exp(gt - gc)).astype(BF16)
        return carry

    lax.fori_loop(0, seq // CUM_TILE, gates_tile, 0, unroll=4)

    jrow = lax.broadcasted_iota(jnp.int32, (SUB, LANES), 0)
    ones_w = jnp.ones((LANES, LANES), BF16)
    n_pieces = c + c // 2
    piece = lax.broadcasted_iota(jnp.int32, (c, n_pieces * SUB), 1) // SUB
    out_row = lax.broadcasted_iota(jnp.int32, (c, n_pieces * SUB), 0)
    row_of_piece = (jnp.where(piece < SUB, piece, SUB + (piece - SUB) // 2),
                    jnp.where(piece < c, piece // 2, piece - SUB))
    sel_both = jnp.concatenate([jnp.where(rp == out_row, 1.0, 0.0) for rp in row_of_piece], axis=1).astype(BF16)

    def scan_step(i, carry):
        tiles, vstacks, row_sets = [], [], []
        for g in range(HGRN_GROUP):
            rows = pl.ds(pl.multiple_of((i * HGRN_GROUP + g) * c, c), c)
            qs = qs_ref[rows, :]
            v = i_ref[0, rows, :].astype(F32)
            parts, vparts = [], []
            for d in range(2):
                kk = kk_ref[d, rows, :]
                gc2 = gc_ref[d, rows, :]
                for r in range(c):
                    for j0 in (0, SUB):
                        if (j0 > r) if d == 0 else (j0 + SUB - 1 < r):
                            continue
                        pair = jnp.exp2(gc2[r:r + 1, :] - gc2[j0:j0 + SUB, :])
                        if not ((j0 + SUB - 1 <= r) if d == 0 else (j0 >= r)):
                            mask = (jrow + j0 <= r) if d == 0 else (jrow + j0 >= r)
                            pair = jnp.where(mask, pair, 0.0)
                        parts.append(pair * kk[j0:j0 + SUB, :] * qs[r:r + 1, :])
                        vparts.append(v[j0:j0 + SUB, :])
            tiles.append(jnp.concatenate(parts, axis=0).astype(BF16))
            vstacks.append(jnp.concatenate(vparts, axis=0))
            row_sets.append(rows)
        a_reps = [_dot(t, ones_w) for t in tiles]
        upds = {}
        for g in range(HGRN_GROUP):
            for d in range(2):
                n = i * HGRN_GROUP + g
                n = n if d == 0 else n_chunks - 1 - n
                rows = pl.ds(pl.multiple_of(n * c, c), c)
                upds[(g, d)] = (rows, lax.dot_general(i_ref[0, rows, :], kd_ref[d, rows, :], (((0,), (0,)), ((), ())),
                                                      preferred_element_type=F32))
        for rows, a_rep, v_stack in zip(row_sets, a_reps, vstacks):
            oi_ref[rows, :] = _dot(sel_both, (a_rep * v_stack).astype(BF16))
        st = list(carry)
        for g in range(HGRN_GROUP):
            for d in range(2):
                rows, upd = upds[(g, d)]
                (of_ref if d == 0 else ob_ref)[rows, :] = _dot_nt(qd_ref[d, rows, :], st[d].astype(BF16))
                e_row = jnp.exp(gt_ref[d, pl.ds(rows.start, 1), :])
                st[d] = st[d] * e_row + upd
        return tuple(st)

    s0 = jnp.zeros((HEAD_W, dk), F32)
    lax.fori_loop(0, n_chunks // HGRN_GROUP, scan_step, (s0, s0))

    def finish(t, carry):
        rows = pl.ds(pl.multiple_of(t * CUM_TILE, CUM_TILE), CUM_TILE)
        o = oi_ref[rows, :] + of_ref[rows, :] + ob_ref[rows, :]
        y = o * lax.rsqrt(jnp.mean(o * o, axis=-1, keepdims=True) + EPS) * ng_ref[...]
        o_ref[0, rows, :] = (y * _silu(z_ref[0, rows, :].astype(F32))).astype(BF16)
        return carry

    lax.fori_loop(0, seq // CUM_TILE, finish, 0, unroll=2)


def _mixer_b(proj, lb, norm_g):
    b, s, _ = proj.shape
    assert s % CUM_TILE == 0 and (s // CHUNK_B) % HGRN_GROUP == 0
    col = lambda cb: pl.BlockSpec((1, s, LANES), lambda bi, h: (bi, 0, cb + h))
    return pl.pallas_call(
        _hgrn_body,
        grid=(b, N_HEADS),
        in_specs=[
            col(CB_BQ), col(CB_BI), col(CB_BF), col(CB_BF + N_HEADS), col(CB_BZ),
            pl.BlockSpec((2, LANES), lambda bi, h: (0, h)),
            pl.BlockSpec((1, LANES), lambda bi, h: (0, 0)),
        ],
        out_specs=pl.BlockSpec((1, s, LANES), lambda bi, h: (bi, 0, h)),
        out_shape=jax.ShapeDtypeStruct((b, s, N_HEADS * HEAD_W), BF16),
        scratch_shapes=[
            pltpu.VMEM((s, LANES), F32),
            pltpu.VMEM((2, s, LANES), F32),
            pltpu.VMEM((2, s, LANES), F32),
            pltpu.VMEM((2, s, LANES), F32),
            pltpu.VMEM((2, s, LANES), BF16),
            pltpu.VMEM((2, s, LANES), BF16),
            pltpu.VMEM((s, LANES), F32),
            pltpu.VMEM((s, LANES), F32),
            pltpu.VMEM((s, LANES), F32),
        ],
        compiler_params=_cparams(("parallel", "arbitrary")),
        name="hgrn",
    )(proj, proj, proj, proj, proj, lb, norm_g)


def _rope_tables(seq):
    half = ROPE_DIM // 2
    inv = 1.0 / (ROPE_THETA ** (jnp.arange(0, ROPE_DIM, 2, dtype=F32) / ROPE_DIM))
    ang = jnp.arange(seq, dtype=F32)[:, None] * inv[None, :]
    cos, sin = jnp.cos(ang), jnp.sin(ang)
    one = jnp.ones((seq, D_C - ROPE_DIM), F32)
    zero = jnp.zeros((seq, D_C - ROPE_DIM), F32)
    zh = jnp.zeros((seq, half), F32)
    c_map = jnp.concatenate([cos, cos, one], axis=1)
    s1_map = jnp.concatenate([zh, sin, zero], axis=1)
    s2_map = jnp.concatenate([-sin, zh, zero], axis=1)
    tile2 = lambda t: jnp.concatenate([t, t], axis=1)
    return tile2(c_map), tile2(s1_map), tile2(s2_map)


def _lower_bounds(lb_logits):
    p = jax.nn.softmax(lb_logits.astype(F32), axis=1)
    return jnp.cumsum(p, axis=1) - p[:, :1]


def kernel(x_prompt, x_sample, norm_g, w_in, conv_w, a_log, dt_bias, gdn_norm_g, hgrn_lb_logits, hgrn_norm_g,
           q_norm_g, k_norm_g, diff_lambda, subln_g, w_br_a, w_br_b, w_br_c, w_out):
    nb_p = x_prompt.shape[0]
    x = jnp.concatenate([x_prompt, x_sample], axis=0)
    b, s, d = x.shape
    depth = w_in.shape[0]
    n_in = w_in.shape[-1]
    gate0 = n_in - 3 * D_MODEL
    w_main = jnp.concatenate([w_in[:, :, gate0:], w_in[:, :, :2048], w_in[:, :, 2048 + N_SMALL:gate0]],
                             axis=-1).astype(BF16)
    w_small = jnp.pad(w_in[:, :, 2048:2048 + N_SMALL], ((0, 0), (0, 0), (0, LANES - N_SMALL))).astype(BF16)
    wa, wb, wc, wo = (w.astype(BF16) for w in (w_br_a, w_br_b, w_br_c, w_out))
    cos_t, s1_t, s2_t = _rope_tables(s)
    lbs = _lower_bounds(hgrn_lb_logits)
    lane_pad = ((0, 0), (2 * N_HEADS, LANES - 4 * N_HEADS))
    neg_a_vec = jnp.pad(-jnp.exp(a_log.astype(F32)).reshape(depth, 2 * N_HEADS), lane_pad)[:, None, :]
    dtb_vec = jnp.pad(dt_bias.astype(F32).reshape(depth, 2 * N_HEADS), lane_pad)[:, None, :]
    lp = diff_lambda.astype(F32)
    lam_dyn = jnp.exp(jnp.sum(lp[:, 0] * lp[:, 1], axis=-1)) - jnp.exp(jnp.sum(lp[:, 2] * lp[:, 3], axis=-1))

    x2d = x.reshape(b * s, d)
    for l in range(depth):
        lambda_init = 0.8 - 0.6 * math.exp(-0.3 * l)
        proj2d, small2d = _inproj(x2d, norm_g[l][None, :], w_main[l], w_small[l])
        proj = proj2d.reshape(b, s, N_MAIN)
        small = small2d.reshape(b, s, LANES)
        small_t = small[:, :, :N_SMALL].transpose(0, 2, 1).reshape(b, N_SMALL, s // CHUNK_A, CHUNK_A)
        ya = _mixer_a(proj, small, small_t, conv_w[l], a_log[l], dt_bias[l], neg_a_vec[l], dtb_vec[l],
                      gdn_norm_g[l][None, :])
        yb = _mixer_b(proj, lbs[:, l], hgrn_norm_g[l][None, :])
        qn, kn = _qkprep(proj, cos_t, s1_t, s2_t, q_norm_g[l].reshape(1, LANES), k_norm_g[l].reshape(1, LANES))
        scal = jnp.stack([lam_dyn[l] + lambda_init, jnp.asarray(1.0 - lambda_init, F32)]).astype(F32)
        yc = _attn(scal, qn, kn, proj, subln_g[l][None, :])
        merge_args = (x2d, ya.reshape(b * s, -1), yb.reshape(b * s, -1), yc.reshape(b * s, -1), proj2d,
                      wa[l], wb[l], wc[l], wo[l])
        if l + 1 < depth:
            x2d = _merge(*merge_args)
    y_prompt = _merge(*merge_args, row0=0, n_rows=nb_p * s)
    y_sample = _merge(*merge_args, row0=nb_p * s, n_rows=(b - nb_p) * s)
    return (y_prompt.reshape(nb_p, s, d), y_sample.reshape(b - nb_p, s, d))
```
